```python
import math
import jax, jax.numpy as jnp
from jax import lax
import numpy as np

D_MODEL = 1024
BATCH = 2
SEQ = 8192
DEPTH = 4

ATTN_HEADS = 16
ATTN_KV_HEADS = 4
ATTN_GROUP = ATTN_HEADS // ATTN_KV_HEADS
ATTN_HEAD_DIM = 64
ATTN_Q_DIM = ATTN_HEADS * ATTN_HEAD_DIM
ATTN_KV_DIM = ATTN_KV_HEADS * ATTN_HEAD_DIM
ATTN_IN_DIM = ATTN_Q_DIM + 2 * ATTN_KV_DIM
WINDOW = 128
ATTN_BLOCK = 128
ROPE_DIM = ATTN_HEAD_DIM // 4
ROPE_THETA = 500000.0

GDN_QK_HEADS = 8
GDN_V_HEADS = 16
GDN_HEAD_K = 128
GDN_HEAD_V = 128
GDN_CONV = 4
GDN_CHUNK = 64
GDN_KEY_DIM = GDN_QK_HEADS * GDN_HEAD_K
GDN_VAL_DIM = GDN_V_HEADS * GDN_HEAD_V
GDN_QKV_DIM = 2 * GDN_KEY_DIM + GDN_VAL_DIM
GDN_IN_DIM = GDN_QKV_DIM + GDN_VAL_DIM + 2 * GDN_V_HEADS

D_FF = 3 * D_MODEL
FFN_CONV = 3

N_ATTN_LAYERS = (DEPTH + 1) // 2
N_GDN_LAYERS = DEPTH // 2
EPS = 1e-6

kernel_name = "hybrid_swa_sink_gdn_convffn"


def rms_norm(x, gain):
    xf = x.astype(jnp.float32)
    y = xf * lax.rsqrt(jnp.mean(xf * xf, axis=-1, keepdims=True) + EPS)
    return (y * gain.astype(jnp.float32)).astype(x.dtype)


def l2_norm(x):
    return x * lax.rsqrt(jnp.sum(x * x, axis=-1, keepdims=True) + EPS)


def causal_dwconv(x, w):
    k = w.shape[0]
    return lax.conv_general_dilated(
        x, w[:, None, :].astype(x.dtype), window_strides=(1,), padding=[(k - 1, 0)],
        dimension_numbers=("NWC", "WIO", "NWC"), feature_group_count=x.shape[-1])


def rope_tables(positions):
    inv_freq = ROPE_THETA ** (-jnp.arange(0, ROPE_DIM, 2, dtype=jnp.float32) / ROPE_DIM)
    ang = positions.astype(jnp.float32)[..., None] * inv_freq
    return jnp.cos(ang)[:, :, None, :], jnp.sin(ang)[:, :, None, :]


def partial_rope(x, cos, sin):
    cos = cos.astype(x.dtype)
    sin = sin.astype(x.dtype)
    half = ROPE_DIM // 2
    x1, x2, xp = x[..., :half], x[..., half:ROPE_DIM], x[..., ROPE_DIM:]
    return jnp.concatenate([x1 * cos - x2 * sin, x2 * cos + x1 * sin, xp], axis=-1)


def swa_sink_attention(u, w_in, q_gain, k_gain, sinks, w_out, cos, sin):
    B, S, _ = u.shape
    nb = S // ATTN_BLOCK
    blk, hkv, g, hd = ATTN_BLOCK, ATTN_KV_HEADS, ATTN_GROUP, ATTN_HEAD_DIM
    qkv = u @ w_in
    q = qkv[..., :ATTN_Q_DIM].reshape(B, S, ATTN_HEADS, hd)
    k = qkv[..., ATTN_Q_DIM:ATTN_Q_DIM + ATTN_KV_DIM].reshape(B, S, hkv, hd)
    v = qkv[..., ATTN_Q_DIM + ATTN_KV_DIM:].reshape(B, S, hkv, hd)
    q = partial_rope(rms_norm(q, q_gain), cos, sin)
    k = partial_rope(rms_norm(k, k_gain), cos, sin)
    qb = q.reshape(B, nb, blk, hkv, g, hd)

    def band(t):
        prev = jnp.pad(t, ((0, 0), (blk, 0), (0, 0), (0, 0)))[:, :S]
        return jnp.concatenate([prev.reshape(B, nb, blk, hkv, hd),
                                t.reshape(B, nb, blk, hkv, hd)], axis=2)

    kb, vb = band(k), band(v)
    scores = jnp.einsum("bnqhgd,bnkhd->bnhgqk", qb, kb,
                        preferred_element_type=jnp.float32) * (hd ** -0.5)
    dist = jnp.arange(blk)[:, None] + blk - jnp.arange(2 * blk)[None, :]
    in_window = (dist >= 0) & (dist < WINDOW)
    key_pos = jnp.arange(nb)[:, None] * blk - blk + jnp.arange(2 * blk)[None, :]
    mask = in_window[None] & (key_pos >= 0)[:, None, :]
    scores = jnp.where(mask[None, :, None, None], scores, jnp.finfo(jnp.float32).min)
    sink = jnp.broadcast_to(sinks.astype(jnp.float32).reshape(1, 1, hkv, g, 1, 1),
                            scores.shape[:-1] + (1,))
    probs = jax.nn.softmax(jnp.concatenate([scores, sink], axis=-1), axis=-1)[..., :-1]
    out = jnp.einsum("bnhgqk,bnkhd->bnqhgd", probs.astype(vb.dtype), vb)
    return out.reshape(B, S, ATTN_Q_DIM) @ w_out


def chunk_gated_delta_rule(q, k, v, g, beta):
    out_dtype = v.dtype
    B, S, H, DK = q.shape
    DV = v.shape[-1]
    C = GDN_CHUNK
    nc = S // C
    f32 = jnp.float32

    def chunks(t):
        t = t.astype(f32).reshape((B, nc, C, H) + t.shape[3:])
        return jnp.moveaxis(t, 3, 1)

    q, k, v, g, beta = chunks(q), chunks(k), chunks(v), chunks(g), chunks(beta)
    gc = jnp.cumsum(g, axis=-1)
    incl = jnp.tril(jnp.ones((C, C), dtype=bool))
    strict = jnp.tril(jnp.ones((C, C), dtype=bool), k=-1)
    decay = jnp.exp(jnp.where(incl, gc[..., :, None] - gc[..., None, :], -jnp.inf))
    kk = jnp.einsum("bhncd,bhnsd->bhncs", k, k)
    m = jnp.where(strict, beta[..., :, None] * kk * decay, 0.0)
    a = m + jnp.eye(C, dtype=f32)
    rhs = jnp.concatenate([v * beta[..., None], k * (beta * jnp.exp(gc))[..., None]], axis=-1)
    sol = lax.linalg.triangular_solve(a, rhs, left_side=True, lower=True, unit_diagonal=True)
    u, w = sol[..., :DV], sol[..., DV:]
    qk = jnp.where(incl, jnp.einsum("bhncd,bhnsd->bhncs", q, k) * decay, 0.0)
    q_dec = q * jnp.exp(gc)[..., None]
    k_dec = k * jnp.exp(gc[..., -1:] - gc)[..., None]
    g_last = jnp.exp(gc[..., -1])

    def step(state, inp):
        u_c, w_c, qk_c, qd_c, kd_c, gl_c = inp
        v_new = u_c - jnp.einsum("bhcd,bhde->bhce", w_c, state)
        o = jnp.einsum("bhcd,bhde->bhce", qd_c, state) + jnp.einsum("bhcs,bhse->bhce", qk_c, v_new)
        state = state * gl_c[..., None, None] + jnp.einsum("bhcd,bhce->bhde", kd_c, v_new)
        return state, o

    xs = tuple(jnp.moveaxis(t, 2, 0) for t in (u, w, qk, q_dec, k_dec, g_last))
    _, o = lax.scan(step, jnp.zeros((B, H, DK, DV), f32), xs)
    o = jnp.transpose(o, (1, 0, 3, 2, 4)).reshape(B, S, H, DV)
    return o.astype(out_dtype)


def gated_deltanet(u, w_in, conv_w, a_log, dt_bias, norm_gain, w_out):
    B, S, _ = u.shape
    proj = u @ w_in
    qkv = jax.nn.silu(causal_dwconv(proj[..., :GDN_QKV_DIM], conv_w))
    z = proj[..., GDN_QKV_DIM:GDN_QKV_DIM + GDN_VAL_DIM].reshape(B, S, GDN_V_HEADS, GDN_HEAD_V)
    b = proj[..., GDN_QKV_DIM + GDN_VAL_DIM:GDN_QKV_DIM + GDN_VAL_DIM + GDN_V_HEADS]
    a = proj[..., GDN_QKV_DIM + GDN_VAL_DIM + GDN_V_HEADS:]
    rep = GDN_V_HEADS // GDN_QK_HEADS
    q = qkv[..., :GDN_KEY_DIM].reshape(B, S, GDN_QK_HEADS, GDN_HEAD_K).astype(jnp.float32)
    k = qkv[..., GDN_KEY_DIM:2 * GDN_KEY_DIM].reshape(B, S, GDN_QK_HEADS, GDN_HEAD_K).astype(jnp.float32)
    v = qkv[..., 2 * GDN_KEY_DIM:].reshape(B, S, GDN_V_HEADS, GDN_HEAD_V)
    q = jnp.repeat(l2_norm(q) * (GDN_HEAD_K ** -0.5), rep, axis=2)
    k = jnp.repeat(l2_norm(k), rep, axis=2)
    beta = jax.nn.sigmoid(b.astype(jnp.float32))
    g = -jnp.exp(a_log.astype(jnp.float32)) * jax.nn.softplus(
        a.astype(jnp.float32) + dt_bias.astype(jnp.float32))
    o = chunk_gated_delta_rule(q, k, v, g, beta)
    o = rms_norm(o, norm_gain) * jax.nn.silu(z)
    return o.reshape(B, S, GDN_VAL_DIM) @ w_out


def conv_glu_ffn(u, w_up, conv_w, conv_b, w_down):
    gu = u @ w_up
    gate = causal_dwconv(gu[..., :D_FF], conv_w) + conv_b
    return (jax.nn.silu(gate) * gu[..., D_FF:]) @ w_down


def setup_inputs(seed: int = 0) -> dict:
    key = jax.random.key(seed)
    ks = jax.random.split(key, 24)
    f32 = jnp.float32

    def nrm(k, shape, scale):
        return jax.random.normal(k, shape, f32) * scale

    x = nrm(ks[0], (BATCH, SEQ, D_MODEL), 1.0)
    start = jax.random.randint(ks[1], (BATCH, 1), 0, 4096, dtype=jnp.int32)
    positions = start + jnp.arange(SEQ, dtype=jnp.int32)[None, :]
    mixer_norm = 1.0 + nrm(ks[2], (DEPTH, D_MODEL), 0.05)
    ffn_norm = 1.0 + nrm(ks[3], (DEPTH, D_MODEL), 0.05)

    attn_w_in = nrm(ks[4], (N_ATTN_LAYERS, D_MODEL, ATTN_IN_DIM), D_MODEL ** -0.5)
    attn_q_gain = 1.0 + nrm(ks[5], (N_ATTN_LAYERS, ATTN_HEAD_DIM), 0.05)
    attn_k_gain = 1.0 + nrm(ks[6], (N_ATTN_LAYERS, ATTN_HEAD_DIM), 0.05)
    attn_sinks = nrm(ks[7], (N_ATTN_LAYERS, ATTN_HEADS), 1.0)
    attn_w_out = nrm(ks[8], (N_ATTN_LAYERS, ATTN_Q_DIM, D_MODEL), ATTN_Q_DIM ** -0.5)

    gdn_w_in = nrm(ks[9], (N_GDN_LAYERS, D_MODEL, GDN_IN_DIM), D_MODEL ** -0.5)
    gdn_conv = nrm(ks[10], (N_GDN_LAYERS, GDN_CONV, GDN_QKV_DIM), GDN_CONV ** -0.5)
    gdn_a_log = jnp.log(jax.random.uniform(ks[11], (N_GDN_LAYERS, GDN_V_HEADS), f32, 1.0, 16.0))
    dt = jnp.exp(jax.random.uniform(ks[12], (N_GDN_LAYERS, GDN_V_HEADS), f32,
                                    math.log(1e-3), math.log(1e-1)))
    gdn_dt_bias = dt + jnp.log(-jnp.expm1(-dt))
    gdn_norm = 1.0 + nrm(ks[13], (N_GDN_LAYERS, GDN_HEAD_V), 0.05)
    gdn_w_out = nrm(ks[14], (N_GDN_LAYERS, GDN_VAL_DIM, D_MODEL), GDN_VAL_DIM ** -0.5)

    ffn_w_up = nrm(ks[15], (DEPTH, D_MODEL, 2 * D_FF), D_MODEL ** -0.5)
    ffn_conv = nrm(ks[16], (DEPTH, FFN_CONV, D_FF), FFN_CONV ** -0.5)
    ffn_conv_b = nrm(ks[17], (DEPTH, D_FF), 0.02)
    ffn_w_down = nrm(ks[18], (DEPTH, D_FF, D_MODEL), D_FF ** -0.5)
    return {"x": x, "positions": positions, "mixer_norm": mixer_norm, "ffn_norm": ffn_norm,
            "attn_w_in": attn_w_in, "attn_q_gain": attn_q_gain, "attn_k_gain": attn_k_gain,
            "attn_sinks": attn_sinks, "attn_w_out": attn_w_out,
            "gdn_w_in": gdn_w_in, "gdn_conv": gdn_conv, "gdn_a_log": gdn_a_log,
            "gdn_dt_bias": gdn_dt_bias, "gdn_norm": gdn_norm, "gdn_w_out": gdn_w_out,
            "ffn_w_up": ffn_w_up, "ffn_conv": ffn_conv, "ffn_conv_b": ffn_conv_b,
            "ffn_w_down": ffn_w_down}


def reference(x, positions, mixer_norm, ffn_norm, attn_w_in, attn_q_gain, attn_k_gain,
              attn_sinks, attn_w_out, gdn_w_in, gdn_conv, gdn_a_log, gdn_dt_bias, gdn_norm,
              gdn_w_out, ffn_w_up, ffn_conv, ffn_conv_b, ffn_w_down):
    cos, sin = rope_tables(positions)
    h = x
    for i in range(DEPTH):
        j = i // 2
        u = rms_norm(h, mixer_norm[i])
        if i % 2 == 0:
            h = h + swa_sink_attention(u, attn_w_in[j], attn_q_gain[j], attn_k_gain[j],
                                       attn_sinks[j], attn_w_out[j], cos, sin)
        else:
            h = h + gated_deltanet(u, gdn_w_in[j], gdn_conv[j], gdn_a_log[j], gdn_dt_bias[j],
                                   gdn_norm[j], gdn_w_out[j])
        u = rms_norm(h, ffn_norm[i])
        h = h + conv_glu_ffn(u, ffn_w_up[i], ffn_conv[i], ffn_conv_b[i], ffn_w_down[i])
    return h
```

```python
import functools

import jax
import jax.numpy as jnp
from jax import lax
from jax.experimental import pallas as pl
from jax.experimental.pallas import tpu as pltpu

F32 = jnp.float32
BF16 = jnp.bfloat16
EPS = 1e-6

D_MODEL = 1024
A_HEADS = 16
A_KV = 4
A_GROUP = A_HEADS // A_KV
A_HD = 64
A_Q = A_HEADS * A_HD
A_KVD = A_KV * A_HD
A_BLK = 128
ROPE_DIM = A_HD // 4
ROPE_THETA = 500000.0
G_QK = 8
G_V = 16
G_DK = 128
G_DV = 128
G_CONV = 4
G_CHUNK = 64
G_KEY = G_QK * G_DK
G_VAL = G_V * G_DV
G_QKV = 2 * G_KEY + G_VAL
G_MAIN = G_QKV + G_VAL
D_FF = 3 * D_MODEL
F_CONV = 3

LANES = 128
SUBLANES = 8
VMEM_LIMIT = 56 * 1024 * 1024
NEG_BIG = -1e30


def _dot(a, b):
    return jnp.dot(a, b, preferred_element_type=F32)


def _dot_nt(a, b):
    return lax.dot_general(a, b, (((1,), (1,)), ((), ())), preferred_element_type=F32)


def _dot_tn(a, b):
    return lax.dot_general(a, b, (((0,), (0,)), ((), ())), preferred_element_type=F32)


def _rms(x, gain):
    ms = jnp.mean(x * x, axis=-1, keepdims=True)
    return x * lax.rsqrt(ms + EPS) * gain


def _silu(x):
    return x * (1.0 / (1.0 + jnp.exp(-x)))


def _split3(x):
    a = x.astype(BF16)
    r = x - a.astype(F32)
    b = r.astype(BF16)
    c = (r - b.astype(F32)).astype(BF16)
    return a, b, c


def _norm_rope_pair(x, gain, cosf, sins, lane):
    x2 = x * x
    lo = lane < A_HD
    s_lo = jnp.sum(jnp.where(lo, x2, 0.0), axis=-1, keepdims=True)
    s_hi = jnp.sum(jnp.where(lo, 0.0, x2), axis=-1, keepdims=True)
    ms = jnp.where(lo, s_lo, s_hi) * (1.0 / A_HD)
    y = x * lax.rsqrt(ms + EPS) * gain
    rot = jnp.where((lane % A_HD) < ROPE_DIM // 2,
                    pltpu.roll(y, LANES - ROPE_DIM // 2, 1), pltpu.roll(y, ROPE_DIM // 2, 1))
    return y * cosf + rot * sins


def _attn_kernel(sink_ref, h_ref, gain_ref, win_ref, wout_ref, cos_ref, sin_ref, qg_ref, kg_ref,
                 out_ref, q_s, k_s, v_s, o_s, *, tile, tiles_per_seq):
    step = pl.program_id(0)
    first = (step % tiles_per_seq) == 0
    nblk = tile // A_BLK

    @pl.when(first)
    def _():
        k_s[0:A_BLK, :] = jnp.zeros((A_BLK, A_KVD), BF16)
        v_s[0:A_BLK, :] = jnp.zeros((A_BLK, A_KVD), BF16)

    @pl.when(jnp.logical_not(first))
    def _():
        k_s[0:A_BLK, :] = k_s[tile:tile + A_BLK, :]
        v_s[0:A_BLK, :] = v_s[tile:tile + A_BLK, :]

    x = h_ref[...]
    xn = _rms(x, gain_ref[...]).astype(BF16)
    qkv = _dot(xn, win_ref[...])

    cosf = cos_ref[...]
    sins = sin_ref[...]
    lane = lax.broadcasted_iota(jnp.int32, (tile, LANES), 1)
    qg = qg_ref[...]
    kg = kg_ref[...]
    for p in range(A_Q // LANES):
        c0 = p * LANES
        qp = _norm_rope_pair(qkv[:, c0:c0 + LANES], qg, cosf, sins, lane)
        q_s[:, c0:c0 + LANES] = (qp * (A_HD ** -0.5)).astype(BF16)
    for p in range(A_KVD // LANES):
        c0 = A_Q + p * LANES
        kp = _norm_rope_pair(qkv[:, c0:c0 + LANES], kg, cosf, sins, lane)
        k_s[A_BLK:A_BLK + tile, p * LANES:(p + 1) * LANES] = kp.astype(BF16)
    v_s[A_BLK:A_BLK + tile, :] = qkv[:, A_Q + A_KVD:].astype(BF16)

    rows = A_GROUP * A_BLK
    qi = lax.broadcasted_iota(jnp.int32, (rows, A_BLK), 0) % A_BLK
    kj = lax.broadcasted_iota(jnp.int32, (rows, A_BLK), 1)
    causal = kj <= qi

    def blk_body(blk, carry):
        r0 = pl.multiple_of(blk * A_BLK, A_BLK)
        has_prev = jnp.logical_or(blk > 0, jnp.logical_not(first))
        for g in range(A_KV):
            q4 = q_s[pl.ds(r0, A_BLK), g * A_GROUP * A_HD:(g + 1) * A_GROUP * A_HD]
            qs = jnp.concatenate([q4[:, j * A_HD:(j + 1) * A_HD] for j in range(A_GROUP)], axis=0)
            kb = k_s[pl.ds(r0, 2 * A_BLK), g * A_HD:(g + 1) * A_HD]
            vb = v_s[pl.ds(r0, 2 * A_BLK), g * A_HD:(g + 1) * A_HD]
            s = _dot_nt(qs, kb)
            s_prev = jnp.where(has_prev, s[:, :A_BLK], NEG_BIG)
            sc = jnp.where(causal, s[:, A_BLK:], s_prev)
            sink = jnp.concatenate(
                [jnp.full((A_BLK, 1), sink_ref[g * A_GROUP + j], F32) for j in range(A_GROUP)],
                axis=0)
            m = jnp.maximum(jnp.max(sc, axis=-1, keepdims=True), sink)
            p = jnp.exp(sc - m)
            denom = jnp.sum(p, axis=-1, keepdims=True) + jnp.exp(sink - m)
            p = p * (1.0 / denom)
            pp = jnp.concatenate([jnp.where(causal, 0.0, p), jnp.where(causal, p, 0.0)],
                                 axis=1).astype(BF16)
            o = _dot(pp, vb)
            o4 = jnp.concatenate([o[j * A_BLK:(j + 1) * A_BLK] for j in range(A_GROUP)], axis=1)
            o_s[pl.ds(r0, A_BLK), g * A_GROUP * A_HD:(g + 1) * A_GROUP * A_HD] = o4.astype(BF16)
        return carry

    lax.fori_loop(0, nblk, blk_body, 0)
    out_ref[...] = x + _dot(o_s[...], wout_ref[...])


def _attn_layer(h, gain, w_in, w_out, cosf, sins, qg, kg, sinks, *, seq, tile):
    m = h.shape[0]
    tiles_per_seq = seq // tile
    kern = functools.partial(_attn_kernel, tile=tile, tiles_per_seq=tiles_per_seq)
    row = lambda i: (i, 0)
    fixed = lambda i: (0, 0)
    return pl.pallas_call(
        kern,
        grid=(m // tile,),
        in_specs=[
            pl.BlockSpec(memory_space=pltpu.SMEM),
            pl.BlockSpec((tile, D_MODEL), row),
            pl.BlockSpec((1, D_MODEL), fixed),
            pl.BlockSpec((D_MODEL, A_Q + 2 * A_KVD), fixed),
            pl.BlockSpec((A_Q, D_MODEL), fixed),
            pl.BlockSpec((tile, LANES), row),
            pl.BlockSpec((tile, LANES), row),
            pl.BlockSpec((1, LANES), fixed),
            pl.BlockSpec((1, LANES), fixed),
        ],
        out_specs=pl.BlockSpec((tile, D_MODEL), row),
        out_shape=jax.ShapeDtypeStruct((m, D_MODEL), F32),
        scratch_shapes=[
            pltpu.VMEM((tile, A_Q), BF16),
            pltpu.VMEM((tile + A_BLK, A_KVD), BF16),
            pltpu.VMEM((tile + A_BLK, A_KVD), BF16),
            pltpu.VMEM((tile, A_Q), BF16),
        ],
        compiler_params=pltpu.CompilerParams(dimension_semantics=("arbitrary",),
                                             vmem_limit_bytes=VMEM_LIMIT),
        name="attn_layer",
    )(sinks, h, gain, w_in, w_out, cosf, sins, qg, kg)


def _causal_conv(vals, ext_s, carry_s, j, first, cw, taps):
    tm = vals.shape[0]

    @pl.when(first)
    def _():
        ext_s[0:SUBLANES, :] = jnp.zeros((SUBLANES, vals.shape[1]), F32)

    @pl.when(jnp.logical_not(first))
    def _():
        ext_s[0:SUBLANES, :] = carry_s[j]

    ext_s[SUBLANES:SUBLANES + tm, :] = vals
    carry_s[j] = vals[tm - SUBLANES:tm, :]
    acc = vals * cw[taps - 1:taps, :]
    for k in range(taps - 1):
        shift = taps - 1 - k
        acc = acc + ext_s[SUBLANES - shift:SUBLANES - shift + tm, :] * cw[k:k + 1, :]
    return acc


def _ffn_kernel(h_ref, gain_ref, wg_ref, wu_ref, cw_ref, cb_ref, wd_ref, out_ref,
                xn_s, ext_s, carry_s, *, tiles_per_seq):
    i = pl.program_id(0)
    j = pl.program_id(1)
    first = (i % tiles_per_seq) == 0

    @pl.when(j == 0)
    def _():
        x = h_ref[...]
        xn_s[...] = _rms(x, gain_ref[...]).astype(BF16)
        out_ref[...] = x

    xn = xn_s[...]
    gate = _dot(xn, wg_ref[...])
    up = _dot(xn, wu_ref[...])
    conv = _causal_conv(gate, ext_s, carry_s, j, first, cw_ref[...], F_CONV) + cb_ref[...]
    act = (_silu(conv) * up).astype(BF16)
    out_ref[...] += _dot(act, wd_ref[...])


def _ffn_layer(h, gain, w_up, conv_w, conv_b, w_down, *, seq, tm, tf):
    m = h.shape[0]
    nj = D_FF // tf
    kern = functools.partial(_ffn_kernel, tiles_per_seq=seq // tm)
    return pl.pallas_call(
        kern,
        grid=(m // tm, nj),
        in_specs=[
            pl.BlockSpec((tm, D_MODEL), lambda i, j: (i, 0)),
            pl.BlockSpec((1, D_MODEL), lambda i, j: (0, 0)),
            pl.BlockSpec((D_MODEL, tf), lambda i, j: (0, j)),
            pl.BlockSpec((D_MODEL, tf), lambda i, j: (0, j + nj)),
            pl.BlockSpec((F_CONV, tf), lambda i, j: (0, j)),
            pl.BlockSpec((1, tf), lambda i, j: (0, j)),
            pl.BlockSpec((tf, D_MODEL), lambda i, j: (j, 0)),
        ],
        out_specs=pl.BlockSpec((tm, D_MODEL), lambda i, j: (i, 0)),
        out_shape=jax.ShapeDtypeStruct((m, D_MODEL), F32),
        scratch_shapes=[
            pltpu.VMEM((tm, D_MODEL), BF16),
            pltpu.VMEM((tm + SUBLANES, tf), F32),
            pltpu.VMEM((nj, SUBLANES, tf), F32),
        ],
        compiler_params=pltpu.CompilerParams(dimension_semantics=("arbitrary", "arbitrary"),
                                             vmem_limit_bytes=VMEM_LIMIT),
        name="ffn_layer",
    )(h, gain, w_up, w_up, conv_w, conv_b, w_down)


def _gdn_in_kernel(h_ref, gain_ref, w_ref, wba_ref, cw_ref, alog_ref, dtb_ref, out_ref, bg_ref,
                   xn_s, ext_s, carry_s, *, tiles_per_seq, n_conv_tiles):
    i = pl.program_id(0)
    j = pl.program_id(1)
    first = (i % tiles_per_seq) == 0

    @pl.when(j == 0)
    def _():
        xn = _rms(h_ref[...], gain_ref[...]).astype(BF16)
        xn_s[...] = xn
        ba = _dot(xn, wba_ref[...])
        lane = lax.broadcasted_iota(jnp.int32, ba.shape, 1)
        beta = 1.0 / (1.0 + jnp.exp(-ba))
        sp_in = ba + dtb_ref[...]
        softplus = jnp.maximum(sp_in, 0.0) + jnp.log1p(jnp.exp(-jnp.abs(sp_in)))
        g = -jnp.exp(alog_ref[...]) * softplus
        bg_ref[...] = jnp.where(lane < G_V, beta, jnp.where(lane < 2 * G_V, g, 0.0))

    proj = _dot(xn_s[...], w_ref[...])

    @pl.when(j < n_conv_tiles)
    def _():
        conv = _causal_conv(proj, ext_s, carry_s, j, first, cw_ref[...], G_CONV)
        out_ref[...] = _silu(conv).astype(BF16)

    @pl.when(j >= n_conv_tiles)
    def _():
        out_ref[...] = proj.astype(BF16)


def _gdn_in_proj(h, gain, w_main, w_ba, conv_w, alog, dtb, *, seq, tm, tn):
    m = h.shape[0]
    n_conv_tiles = G_QKV // tn
    kern = functools.partial(_gdn_in_kernel, tiles_per_seq=seq // tm, n_conv_tiles=n_conv_tiles)
    return pl.pallas_call(
        kern,
        grid=(m // tm, G_MAIN // tn),
        in_specs=[
            pl.BlockSpec((tm, D_MODEL), lambda i, j: (i, 0)),
            pl.BlockSpec((1, D_MODEL), lambda i, j: (0, 0)),
            pl.BlockSpec((D_MODEL, tn), lambda i, j: (0, j)),
            pl.BlockSpec((D_MODEL, LANES), lambda i, j: (0, 0)),
            pl.BlockSpec((G_CONV, tn), lambda i, j: (0, jnp.minimum(j, n_conv_tiles - 1))),
            pl.BlockSpec((1, LANES), lambda i, j: (0, 0)),
            pl.BlockSpec((1, LANES), lambda i, j: (0, 0)),
        ],
        out_specs=[
            pl.BlockSpec((tm, tn), lambda i, j: (i, j)),
            pl.BlockSpec((tm, LANES), lambda i, j: (i, 0)),
        ],
        out_shape=[
            jax.ShapeDtypeStruct((m, G_MAIN), BF16),
            jax.ShapeDtypeStruct((m, LANES), F32),
        ],
        scratch_shapes=[
            pltpu.VMEM((tm, D_MODEL), BF16),
            pltpu.VMEM((tm + SUBLANES, tn), F32),
            pltpu.VMEM((n_conv_tiles, SUBLANES, tn), F32),
        ],
        compiler_params=pltpu.CompilerParams(dimension_semantics=("arbitrary", "arbitrary"),
                                             vmem_limit_bytes=VMEM_LIMIT),
        name="gdn_in_proj",
    )(h, gain, w_main, w_ba, conv_w, alog, dtb)


G_PAIR = G_V // G_QK


def _gdn_chunk_kernel(q_ref, k_ref, v_ref, z_ref, bg_ref, bgt_ref, ng_ref, out_ref, st_s, *, tc):
    hg = pl.program_id(1)
    c = pl.program_id(2)
    nch = tc // G_CHUNK

    @pl.when(c == 0)
    def _():
        st_s[...] = jnp.zeros(st_s.shape, F32)

    ri = lax.broadcasted_iota(jnp.int32, (tc, tc), 0)
    ci = lax.broadcasted_iota(jnp.int32, (tc, tc), 1)
    same = (ri // G_CHUNK) == (ci // G_CHUNK)
    incl = jnp.logical_and(same, ci <= ri)
    strict = jnp.logical_and(same, ci < ri)
    ones_incl = jnp.where(incl, 1.0, 0.0).astype(BF16)
    ones_same = jnp.where(same, 1.0, 0.0).astype(BF16)
    ones_inclT = jnp.where(jnp.logical_and(same, ri <= ci), 1.0, 0.0).astype(BF16)
    eye = jnp.where(ri == ci, 1.0, 0.0)

    bg = bg_ref[...]
    bgt = bgt_ref[...]
    gc_all = jnp.zeros((tc, LANES), F32)
    gt_all = jnp.zeros((tc, LANES), F32)
    for piece in _split3(bg):
        gc_all = gc_all + _dot(ones_incl, piece)
        gt_all = gt_all + _dot(ones_same, piece)
    gcr_all = jnp.zeros(bgt.shape, F32)
    for piece in _split3(bgt):
        gcr_all = gcr_all + _dot(piece, ones_inclT)

    q = q_ref[...].astype(F32)
    k = k_ref[...].astype(F32)
    qn = q * lax.rsqrt(jnp.sum(q * q, axis=-1, keepdims=True) + EPS) * (G_DK ** -0.5)
    kn = k * lax.rsqrt(jnp.sum(k * k, axis=-1, keepdims=True) + EPS)
    qb = qn.astype(BF16)
    kb = kn.astype(BF16)
    kk = _dot_nt(kb, kb)
    qk = _dot_nt(qb, kb)

    lane = lax.broadcasted_iota(jnp.int32, (tc, LANES), 1)
    srow = lax.broadcasted_iota(jnp.int32, bgt.shape, 0)
    ngain = ng_ref[...]

    for hh in range(G_PAIR):
        head = hg * G_PAIR + hh
        pick = lambda a, off: jnp.sum(jnp.where(lane == head + off, a, 0.0), axis=-1, keepdims=True)
        beta_c = pick(bg, 0)
        gc_c = pick(gc_all, G_V)
        gt_c = pick(gt_all, G_V)
        gc_r = jnp.sum(jnp.where(srow == head + G_V, gcr_all, 0.0), axis=0, keepdims=True)
        egc = jnp.exp(gc_c)
        decay = jnp.exp(jnp.where(incl, gc_c - gc_r, NEG_BIG))
        neg_m = jnp.where(strict, -(beta_c * kk * decay), 0.0)
        tinv = eye + neg_m
        npow = neg_m.astype(BF16)
        for _ in range(5):
            npow_f = _dot(npow, npow)
            npow = npow_f.astype(BF16)
            tinv = tinv + _dot(tinv.astype(BF16), npow)
        v = v_ref[:, hh * G_DV:(hh + 1) * G_DV].astype(F32)
        rhs = jnp.concatenate([v * beta_c, kn * (beta_c * egc)], axis=1).astype(BF16)
        uw = _dot(tinv.astype(BF16), rhs)
        u = uw[:, :G_DV]
        w = uw[:, G_DV:].astype(BF16)
        qkm = jnp.where(incl, qk * decay, 0.0).astype(BF16)
        qd = (qn * egc).astype(BF16)
        kd = (kn * jnp.exp(gt_c - gc_c)).astype(BF16)
        glast = jnp.exp(gt_c)

        state = st_s[hh]
        o_state = []
        v_new = []
        for cc in range(nch):
            r0, r1 = cc * G_CHUNK, (cc + 1) * G_CHUNK
            sb = state.astype(BF16)
            vn = u[r0:r1] - _dot(w[r0:r1], sb)
            o_state.append(_dot(qd[r0:r1], sb))
            vnb = vn.astype(BF16)
            v_new.append(vnb)
            state = state * glast[r0:r0 + 1, :] + _dot_tn(kd[r0:r1], vnb)
        st_s[hh] = state
        o = jnp.concatenate(o_state, axis=0) + _dot(qkm, jnp.concatenate(v_new, axis=0))
        z = z_ref[:, hh * G_DV:(hh + 1) * G_DV].astype(F32)
        y = _rms(o, ngain) * _silu(z)
        out_ref[:, hh * G_DV:(hh + 1) * G_DV] = y.astype(BF16)


def _gdn_chunk(pa, bg, bgt, ngain, *, batch, seq, tc):
    m = pa.shape[0]
    nc = seq // tc
    kern = functools.partial(_gdn_chunk_kernel, tc=tc)
    pair_w = G_PAIR * G_DV
    rowi = lambda b, g, c: b * nc + c
    return pl.pallas_call(
        kern,
        grid=(batch, G_QK, nc),
        in_specs=[
            pl.BlockSpec((tc, G_DK), lambda b, g, c: (rowi(b, g, c), g)),
            pl.BlockSpec((tc, G_DK), lambda b, g, c: (rowi(b, g, c), G_QK + g)),
            pl.BlockSpec((tc, pair_w), lambda b, g, c: (rowi(b, g, c), 2 * G_KEY // pair_w + g)),
            pl.BlockSpec((tc, pair_w), lambda b, g, c: (rowi(b, g, c), G_QKV // pair_w + g)),
            pl.BlockSpec((tc, LANES), lambda b, g, c: (rowi(b, g, c), 0)),
            pl.BlockSpec((None, 2 * G_V, tc), lambda b, g, c: (b, 0, c)),
            pl.BlockSpec((1, G_DV), lambda b, g, c: (0, 0)),
        ],
        out_specs=pl.BlockSpec((tc, pair_w), lambda b, g, c: (rowi(b, g, c), g)),
        out_shape=jax.ShapeDtypeStruct((m, G_VAL), BF16),
        scratch_shapes=[pltpu.VMEM((G_PAIR, G_DK, G_DV), F32)],
        compiler_params=pltpu.CompilerParams(
            dimension_semantics=("arbitrary", "arbitrary", "arbitrary"),
            vmem_limit_bytes=VMEM_LIMIT),
        name="gdn_chunk",
    )(pa, pa, pa, pa, bg, bgt, ngain)


def _out_proj_kernel(a_ref, w_ref, h_ref, out_ref):
    out_ref[...] = h_ref[...] + _dot(a_ref[...], w_ref[...])


def _out_proj(a, w, h, *, tm):
    m, kdim = a.shape
    return pl.pallas_call(
        _out_proj_kernel,
        grid=(m // tm,),
        in_specs=[
            pl.BlockSpec((tm, kdim), lambda i: (i, 0)),
            pl.BlockSpec((kdim, D_MODEL), lambda i: (0, 0)),
            pl.BlockSpec((tm, D_MODEL), lambda i: (i, 0)),
        ],
        out_specs=pl.BlockSpec((tm, D_MODEL), lambda i: (i, 0)),
        out_shape=jax.ShapeDtypeStruct((m, D_MODEL), F32),
        compiler_params=pltpu.CompilerParams(dimension_semantics=("arbitrary",),
                                             vmem_limit_bytes=VMEM_LIMIT),
        name="out_proj",
    )(a, w, h)


def _rope_tables(positions):
    half = ROPE_DIM // 2
    inv_freq = ROPE_THETA ** (-jnp.arange(0, ROPE_DIM, 2, dtype=F32) / ROPE_DIM)
    ang = positions.astype(F32).reshape(-1, 1) * inv_freq
    cos, sin = jnp.cos(ang), jnp.sin(ang)
    n = ang.shape[0]
    cos_h = jnp.concatenate([cos, cos, jnp.ones((n, A_HD - 2 * half), F32)], axis=1)
    sin_h = jnp.concatenate([-sin, sin, jnp.zeros((n, A_HD - 2 * half), F32)], axis=1)
    return jnp.tile(cos_h, (1, LANES // A_HD)), jnp.tile(sin_h, (1, LANES // A_HD))


def kernel(x, positions, mixer_norm, ffn_norm, attn_w_in, attn_q_gain, attn_k_gain, attn_sinks,
           attn_w_out, gdn_w_in, gdn_conv, gdn_a_log, gdn_dt_bias, gdn_norm, gdn_w_out,
           ffn_w_up, ffn_conv, ffn_conv_b, ffn_w_down):
    batch, seq, _ = x.shape
    depth = mixer_norm.shape[0]
    m = batch * seq
    attn_tile = min(512, seq)
    row_tile = min(1024, seq)
    gdn_tc = min(256, seq)

    cosf, sins = _rope_tables(positions)
    h = x.reshape(m, D_MODEL)
    for i in range(depth):
        j = i // 2
        gain = mixer_norm[i].reshape(1, D_MODEL)
        if i % 2 == 0:
            qg = jnp.tile(attn_q_gain[j].reshape(1, A_HD), (1, LANES // A_HD))
            kg = jnp.tile(attn_k_gain[j].reshape(1, A_HD), (1, LANES // A_HD))
            h = _attn_layer(h, gain, attn_w_in[j].astype(BF16), attn_w_out[j].astype(BF16),
                            cosf, sins, qg, kg, attn_sinks[j].astype(F32),
                            seq=seq, tile=attn_tile)
        else:
            w_in = gdn_w_in[j]
            w_main = w_in[:, :G_MAIN].astype(BF16)
            w_ba = jnp.pad(w_in[:, G_MAIN:], ((0, 0), (0, LANES - 2 * G_V))).astype(BF16)
            pad16 = lambda a: jnp.pad(a.astype(F32).reshape(1, G_V), ((0, 0), (G_V, LANES - 2 * G_V)))
            pa, bg = _gdn_in_proj(h, gain, w_main, w_ba, gdn_conv[j], pad16(gdn_a_log[j]),
                                  pad16(gdn_dt_bias[j]), seq=seq, tm=row_tile, tn=1024)
            bgt = jnp.transpose(bg.reshape(batch, seq, LANES)[:, :, :2 * G_V], (0, 2, 1))
            o = _gdn_chunk(pa, bg, bgt, gdn_norm[j].reshape(1, G_DV), batch=batch, seq=seq,
                           tc=gdn_tc)
            h = _out_proj(o, gdn_w_out[j].astype(BF16), h, tm=min(512, seq))
        h = _ffn_layer(h, ffn_norm[i].reshape(1, D_MODEL), ffn_w_up[i].astype(BF16), ffn_conv[i],
                       ffn_conv_b[i].reshape(1, D_FF), ffn_w_down[i].astype(BF16),
                       seq=seq, tm=row_tile, tf=512)
    return h.reshape(batch, seq, D_MODEL)
```

```python
import functools

import jax
import jax.numpy as jnp
from jax import lax
from jax.experimental import pallas as pl
from jax.experimental.pallas import tpu as pltpu

F32 = jnp.float32
BF16 = jnp.bfloat16
EPS = 1e-6

D_MODEL = 1024
A_HEADS = 16
A_KV = 4
A_GROUP = A_HEADS // A_KV
A_HD = 64
A_Q = A_HEADS * A_HD
A_KVD = A_KV * A_HD
A_BLK = 128
ROPE_DIM = A_HD // 4
ROPE_THETA = 500000.0
G_QK = 8
G_V = 16
G_DK = 128
G_DV = 128
G_CONV = 4
G_CHUNK = 64
G_KEY = G_QK * G_DK
G_VAL = G_V * G_DV
G_QKV = 2 * G_KEY + G_VAL
G_MAIN = G_QKV + G_VAL
D_FF = 3 * D_MODEL
F_CONV = 3

LANES = 128
SUBLANES = 8
VMEM_LIMIT = 56 * 1024 * 1024
NEG_BIG = -1e30


def _dot(a, b):
    return jnp.dot(a, b, preferred_element_type=F32)


def _dot_nt(a, b):
    return lax.dot_general(a, b, (((1,), (1,)), ((), ())), preferred_element_type=F32)


def _dot_tn(a, b):
    return lax.dot_general(a, b, (((0,), (0,)), ((), ())), preferred_element_type=F32)


def _rms(x, gain):
    ms = jnp.mean(x * x, axis=-1, keepdims=True)
    return x * lax.rsqrt(ms + EPS) * gain


def _silu(x):
    return x * (1.0 / (1.0 + jnp.exp(-x)))


def _split3(x):
    a = x.astype(BF16)
    r = x - a.astype(F32)
    b = r.astype(BF16)
    c = (r - b.astype(F32)).astype(BF16)
    return a, b, c


def _norm_rope_pair(x, gain, cosf, sins, lane):
    x2 = x * x
    lo = lane < A_HD
    s_lo = jnp.sum(jnp.where(lo, x2, 0.0), axis=-1, keepdims=True)
    s_hi = jnp.sum(jnp.where(lo, 0.0, x2), axis=-1, keepdims=True)
    ms = jnp.where(lo, s_lo, s_hi) * (1.0 / A_HD)
    y = x * lax.rsqrt(ms + EPS) * gain
    rot = jnp.where((lane % A_HD) < ROPE_DIM // 2,
                    pltpu.roll(y, LANES - ROPE_DIM // 2, 1), pltpu.roll(y, ROPE_DIM // 2, 1))
    return y * cosf + rot * sins


def _attn_kernel(sink_ref, h_ref, gain_ref, win_ref, wout_ref, cos_ref, sin_ref, qg_ref, kg_ref,
                 out_ref, q_s, klo_s, khi_s, vlo_s, vhi_s, o_s, *, tile, tiles_per_seq):
    step = pl.program_id(0)
    first = (step % tiles_per_seq) == 0
    nblk = tile // A_BLK
    kv_bufs = (klo_s, khi_s, vlo_s, vhi_s)

    @pl.when(first)
    def _():
        for buf in kv_bufs:
            buf[0:A_BLK, :] = jnp.zeros((A_BLK, A_KV * LANES), BF16)

    @pl.when(jnp.logical_not(first))
    def _():
        for buf in kv_bufs:
            buf[0:A_BLK, :] = buf[tile:tile + A_BLK, :]

    x = h_ref[...]
    xn = _rms(x, gain_ref[...]).astype(BF16)
    qkv = _dot(xn, win_ref[...])

    cosf = cos_ref[...]
    sins = sin_ref[...]
    lane = lax.broadcasted_iota(jnp.int32, (tile, LANES), 1)
    low = lane < A_HD
    qg = qg_ref[...]
    kg = kg_ref[...]
    for p in range(A_Q // LANES):
        c0 = p * LANES
        qp = _norm_rope_pair(qkv[:, c0:c0 + LANES], qg, cosf, sins, lane)
        q_s[:, c0:c0 + LANES] = (qp * (A_HD ** -0.5)).astype(BF16)

    def place(pair, lo_s, hi_s, p):
        swapped = pltpu.roll(pair, A_HD, 1)
        rows = slice(A_BLK, A_BLK + tile)
        g0, g1 = 2 * p, 2 * p + 1
        lo_s[rows, g0 * LANES:(g0 + 1) * LANES] = jnp.where(low, pair, 0.0).astype(BF16)
        hi_s[rows, g0 * LANES:(g0 + 1) * LANES] = jnp.where(low, 0.0, swapped).astype(BF16)
        lo_s[rows, g1 * LANES:(g1 + 1) * LANES] = jnp.where(low, swapped, 0.0).astype(BF16)
        hi_s[rows, g1 * LANES:(g1 + 1) * LANES] = jnp.where(low, 0.0, pair).astype(BF16)

    for p in range(A_KVD // LANES):
        c0 = A_Q + p * LANES
        place(_norm_rope_pair(qkv[:, c0:c0 + LANES], kg, cosf, sins, lane), klo_s, khi_s, p)
        place(qkv[:, c0 + A_KVD:c0 + A_KVD + LANES], vlo_s, vhi_s, p)

    rows2 = 2 * A_BLK
    qi = lax.broadcasted_iota(jnp.int32, (rows2, A_BLK), 0) % A_BLK
    kj = lax.broadcasted_iota(jnp.int32, (rows2, A_BLK), 1)
    causal = kj <= qi
    ones = jnp.ones((2 * A_BLK, LANES), BF16)

    for blk in range(nblk):
        r0 = blk * A_BLK
        has_prev = jnp.logical_not(first) if blk == 0 else None
        for g in range(A_KV):
            c0 = g * A_GROUP * A_HD
            qst = jnp.concatenate([q_s[r0:r0 + A_BLK, c0:c0 + LANES],
                                   q_s[r0:r0 + A_BLK, c0 + LANES:c0 + 2 * LANES]], axis=0)
            acc = None
            for par, (k_s, v_s) in enumerate(((klo_s, vlo_s), (khi_s, vhi_s))):
                kb = k_s[r0:r0 + 2 * A_BLK, g * LANES:(g + 1) * LANES]
                vb = v_s[r0:r0 + 2 * A_BLK, g * LANES:(g + 1) * LANES]
                s = _dot_nt(qst, kb)
                s_prev = s[:, :A_BLK]
                if has_prev is not None:
                    s_prev = jnp.where(has_prev, s_prev, NEG_BIG)
                sc = jnp.where(causal, s[:, A_BLK:], s_prev)
                sink = jnp.concatenate(
                    [jnp.full((A_BLK, 1), sink_ref[g * A_GROUP + 2 * pr + par], F32)
                     for pr in range(2)], axis=0)
                m = jnp.maximum(jnp.max(sc, axis=-1, keepdims=True), sink)
                p = jnp.exp(sc - m)
                pp = jnp.concatenate([jnp.where(causal, 0.0, p), jnp.where(causal, p, 0.0)],
                                     axis=1).astype(BF16)
                r = _dot(pp, jnp.concatenate([vb, ones], axis=1))
                o = r[:, :LANES] * (1.0 / (r[:, LANES:] + jnp.exp(sink - m)))
                acc = o if acc is None else acc + o
            o_s[r0:r0 + A_BLK, c0:c0 + LANES] = acc[:A_BLK].astype(BF16)
            o_s[r0:r0 + A_BLK, c0 + LANES:c0 + 2 * LANES] = acc[A_BLK:].astype(BF16)

    out_ref[...] = x + _dot(o_s[...], wout_ref[...])


def _attn_layer(h, gain, w_in, w_out, cosf, sins, qg, kg, sinks, *, seq, tile):
    m = h.shape[0]
    tiles_per_seq = seq // tile
    kern = functools.partial(_attn_kernel, tile=tile, tiles_per_seq=tiles_per_seq)
    row = lambda i: (i, 0)
    fixed = lambda i: (0, 0)
    return pl.pallas_call(
        kern,
        grid=(m // tile,),
        in_specs=[
            pl.BlockSpec(memory_space=pltpu.SMEM),
            pl.BlockSpec((tile, D_MODEL), row),
            pl.BlockSpec((1, D_MODEL), fixed),
            pl.BlockSpec((D_MODEL, A_Q + 2 * A_KVD), fixed),
            pl.BlockSpec((A_Q, D_MODEL), fixed),
            pl.BlockSpec((tile, LANES), row),
            pl.BlockSpec((tile, LANES), row),
            pl.BlockSpec((1, LANES), fixed),
            pl.BlockSpec((1, LANES), fixed),
        ],
        out_specs=pl.BlockSpec((tile, D_MODEL), row),
        out_shape=jax.ShapeDtypeStruct((m, D_MODEL), F32),
        scratch_shapes=[
            pltpu.VMEM((tile, A_Q), BF16),
            pltpu.VMEM((tile + A_BLK, A_KV * LANES), BF16),
            pltpu.VMEM((tile + A_BLK, A_KV * LANES), BF16),
            pltpu.VMEM((tile + A_BLK, A_KV * LANES), BF16),
            pltpu.VMEM((tile + A_BLK, A_KV * LANES), BF16),
            pltpu.VMEM((tile, A_Q), BF16),
        ],
        compiler_params=pltpu.CompilerParams(dimension_semantics=("arbitrary",),
                                             vmem_limit_bytes=VMEM_LIMIT),
        name="attn_layer",
    )(sinks, h, gain, w_in, w_out, cosf, sins, qg, kg)


def _causal_conv(vals, ext_s, carry_s, j, first, cw, taps):
    tm = vals.shape[0]

    @pl.when(first)
    def _():
        ext_s[0:SUBLANES, :] = jnp.zeros((SUBLANES, vals.shape[1]), F32)

    @pl.when(jnp.logical_not(first))
    def _():
        ext_s[0:SUBLANES, :] = carry_s[j]

    ext_s[SUBLANES:SUBLANES + tm, :] = vals
    carry_s[j] = vals[tm - SUBLANES:tm, :]
    acc = vals * cw[taps - 1:taps, :]
    for k in range(taps - 1):
        shift = taps - 1 - k
        acc = acc + ext_s[SUBLANES - shift:SUBLANES - shift + tm, :] * cw[k:k + 1, :]
    return acc


def _ffn_kernel(h_ref, gain_ref, wg_ref, wu_ref, cw_ref, cb_ref, wd_ref, out_ref,
                xn_s, ext_s, carry_s, *, tiles_per_seq):
    i = pl.program_id(0)
    j = pl.program_id(1)
    first = (i % tiles_per_seq) == 0

    @pl.when(j == 0)
    def _():
        x = h_ref[...]
        xn_s[...] = _rms(x, gain_ref[...]).astype(BF16)
        out_ref[...] = x

    xn = xn_s[...]
    gate = _dot(xn, wg_ref[...])
    up = _dot(xn, wu_ref[...])
    conv = _causal_conv(gate, ext_s, carry_s, j, first, cw_ref[...], F_CONV) + cb_ref[...]
    act = (_silu(conv) * up).astype(BF16)
    out_ref[...] += _dot(act, wd_ref[...])


def _ffn_layer(h, gain, w_up, conv_w, conv_b, w_down, *, seq, tm, tf):
    m = h.shape[0]
    nj = D_FF // tf
    kern = functools.partial(_ffn_kernel, tiles_per_seq=seq // tm)
    return pl.pallas_call(
        kern,
        grid=(m // tm, nj),
        in_specs=[
            pl.BlockSpec((tm, D_MODEL), lambda i, j: (i, 0)),
            pl.BlockSpec((1, D_MODEL), lambda i, j: (0, 0)),
            pl.BlockSpec((D_MODEL, tf), lambda i, j: (0, j)),
            pl.BlockSpec((D_MODEL, tf), lambda i, j: (0, j + nj)),
            pl.BlockSpec((F_CONV, tf), lambda i, j: (0, j)),
            pl.BlockSpec((1, tf), lambda i, j: (0, j)),
            pl.BlockSpec((tf, D_MODEL), lambda i, j: (j, 0)),
        ],
        out_specs=pl.BlockSpec((tm, D_MODEL), lambda i, j: (i, 0)),
        out_shape=jax.ShapeDtypeStruct((m, D_MODEL), F32),
        scratch_shapes=[
            pltpu.VMEM((tm, D_MODEL), BF16),
            pltpu.VMEM((tm + SUBLANES, tf), F32),
            pltpu.VMEM((nj, SUBLANES, tf), F32),
        ],
        compiler_params=pltpu.CompilerParams(dimension_semantics=("arbitrary", "arbitrary"),
                                             vmem_limit_bytes=VMEM_LIMIT),
        name="ffn_layer",
    )(h, gain, w_up, w_up, conv_w, conv_b, w_down)


def _gdn_in_kernel(h_ref, gain_ref, w_ref, wba_ref, cw_ref, alog_ref, dtb_ref, out_ref, bg_ref,
                   xn_s, ext_s, carry_s, *, tiles_per_seq, n_conv_tiles):
    i = pl.program_id(0)
    j = pl.program_id(1)
    first = (i % tiles_per_seq) == 0

    @pl.when(j == 0)
    def _():
        xn = _rms(h_ref[...], gain_ref[...]).astype(BF16)
        xn_s[...] = xn
        ba = _dot(xn, wba_ref[...])
        lane = lax.broadcasted_iota(jnp.int32, ba.shape, 1)
        beta = 1.0 / (1.0 + jnp.exp(-ba))
        sp_in = ba + dtb_ref[...]
        softplus = jnp.maximum(sp_in, 0.0) + jnp.log1p(jnp.exp(-jnp.abs(sp_in)))
        g = -jnp.exp(alog_ref[...]) * softplus
        bg_ref[...] = jnp.where(lane < G_V, beta, jnp.where(lane < 2 * G_V, g, 0.0))

    proj = _dot(xn_s[...], w_ref[...])

    @pl.when(j < n_conv_tiles)
    def _():
        conv = _causal_conv(proj, ext_s, carry_s, j, first, cw_ref[...], G_CONV)
        out_ref[...] = _silu(conv).astype(BF16)

    @pl.when(j >= n_conv_tiles)
    def _():
        out_ref[...] = proj.astype(BF16)


def _gdn_in_proj(h, gain, w_main, w_ba, conv_w, alog, dtb, *, seq, tm, tn):
    m = h.shape[0]
    n_conv_tiles = G_QKV // tn
    kern = functools.partial(_gdn_in_kernel, tiles_per_seq=seq // tm, n_conv_tiles=n_conv_tiles)
    return pl.pallas_call(
        kern,
        grid=(m // tm, G_MAIN // tn),
        in_specs=[
            pl.BlockSpec((tm, D_MODEL), lambda i, j: (i, 0)),
            pl.BlockSpec((1, D_MODEL), lambda i, j: (0, 0)),
            pl.BlockSpec((D_MODEL, tn), lambda i, j: (0, j)),
            pl.BlockSpec((D_MODEL, LANES), lambda i, j: (0, 0)),
            pl.BlockSpec((G_CONV, tn), lambda i, j: (0, jnp.minimum(j, n_conv_tiles - 1))),
            pl.BlockSpec((1, LANES), lambda i, j: (0, 0)),
            pl.BlockSpec((1, LANES), lambda i, j: (0, 0)),
        ],
        out_specs=[
            pl.BlockSpec((tm, tn), lambda i, j: (i, j)),
            pl.BlockSpec((tm, LANES), lambda i, j: (i, 0)),
        ],
        out_shape=[
            jax.ShapeDtypeStruct((m, G_MAIN), BF16),
            jax.ShapeDtypeStruct((m, LANES), F32),
        ],
        scratch_shapes=[
            pltpu.VMEM((tm, D_MODEL), BF16),
            pltpu.VMEM((tm + SUBLANES, tn), F32),
            pltpu.VMEM((n_conv_tiles, SUBLANES, tn), F32),
        ],
        compiler_params=pltpu.CompilerParams(dimension_semantics=("arbitrary", "arbitrary"),
                                             vmem_limit_bytes=VMEM_LIMIT),
        name="gdn_in_proj",
    )(h, gain, w_main, w_ba, conv_w, alog, dtb)


G_PAIR = G_V // G_QK


G_WIDE = 256
G_CPB = G_WIDE // G_CHUNK


def _to_wide(full, blk):
    out = full[0:G_CHUNK]
    for cc in range(1, G_CPB):
        out = jnp.where(blk == cc, full[cc * G_CHUNK:(cc + 1) * G_CHUNK], out)
    return out


def _col_to_wide(col, blk):
    out = col[0:G_CHUNK]
    for cc in range(1, G_CPB):
        out = jnp.where(blk == cc, col[cc * G_CHUNK:(cc + 1) * G_CHUNK], out)
    return out


def _block_diag(wide, blk):
    zero = jnp.zeros_like(wide)
    return jnp.concatenate([jnp.where(blk == cc, wide, zero) for cc in range(G_CPB)], axis=0)


def _gdn_chunk_kernel(q_ref, k_ref, v_ref, z_ref, bg_ref, bgt_ref, ng_ref, out_ref, st_s, rows_s,
                      *, tc, heads):
    hg = pl.program_id(1)
    c = pl.program_id(2)
    nblk = tc // G_WIDE

    @pl.when(c == 0)
    def _():
        st_s[...] = jnp.zeros(st_s.shape, F32)

    ri = lax.broadcasted_iota(jnp.int32, (G_WIDE, G_WIDE), 0)
    ci = lax.broadcasted_iota(jnp.int32, (G_WIDE, G_WIDE), 1)
    same = (ri // G_CHUNK) == (ci // G_CHUNK)
    ones_incl = jnp.where(jnp.logical_and(same, ci <= ri), 1.0, 0.0).astype(BF16)
    ones_inclT = jnp.where(jnp.logical_and(same, ri <= ci), 1.0, 0.0).astype(BF16)
    ones_same = jnp.where(same, 1.0, 0.0).astype(BF16)

    wi = lax.broadcasted_iota(jnp.int32, (G_CHUNK, G_WIDE), 0)
    wl = lax.broadcasted_iota(jnp.int32, (G_CHUNK, G_WIDE), 1)
    wblk = wl // G_CHUNK
    wj = wl % G_CHUNK
    incl_w = wj <= wi
    strict_w = wj < wi
    eye_w = jnp.where(wj == wi, 1.0, 0.0)
    kblk = lax.broadcasted_iota(jnp.int32, (G_DK, G_WIDE), 1) // G_CHUNK

    bg = bg_ref[...]
    bgt = bgt_ref[...]
    gc_cols, gc_rows, gt_rows = [], [], []
    for b in range(nblk):
        sl = slice(b * G_WIDE, (b + 1) * G_WIDE)
        gcc = jnp.zeros((G_WIDE, LANES), F32)
        for piece in _split3(bg[sl]):
            gcc = gcc + _dot(ones_incl, piece)
        gc_cols.append(gcc)
        gcr = jnp.zeros((2 * G_V, G_WIDE), F32)
        gtr = jnp.zeros((2 * G_V, G_WIDE), F32)
        for piece in _split3(bgt[:, sl]):
            gcr = gcr + _dot(piece, ones_inclT)
            gtr = gtr + _dot(piece, ones_same)
        gc_rows.append(gcr)
        gt_rows.append(gtr)
    gc_all = jnp.concatenate(gc_cols, axis=0)
    rows_s[0:2 * G_V, :] = bgt
    rows_s[2 * G_V:4 * G_V, :] = jnp.concatenate(gc_rows, axis=1)
    rows_s[4 * G_V:6 * G_V, :] = jnp.concatenate(gt_rows, axis=1)

    lane = lax.broadcasted_iota(jnp.int32, (tc, LANES), 1)
    ngain = ng_ref[...]

    qk_heads = []
    for qh in range(heads // G_PAIR):
        q = q_ref[:, qh * G_DK:(qh + 1) * G_DK].astype(F32)
        k = k_ref[:, qh * G_DK:(qh + 1) * G_DK].astype(F32)
        qn = q * lax.rsqrt(jnp.sum(q * q, axis=-1, keepdims=True) + EPS) * (G_DK ** -0.5)
        kn = k * lax.rsqrt(jnp.sum(k * k, axis=-1, keepdims=True) + EPS)
        qb = qn.astype(BF16)
        kb = kn.astype(BF16)
        kn_t = kn.T
        kk_w, qk_w = [], []
        for b in range(nblk):
            sl = slice(b * G_WIDE, (b + 1) * G_WIDE)
            kk_w.append(_to_wide(_dot_nt(kb[sl], kb[sl]), wblk))
            qk_w.append(_to_wide(_dot_nt(qb[sl], kb[sl]), wblk))
        qk_heads.append((qn, kb, kn_t, kk_w, qk_w))

    per_head = []
    chains = []
    for hh in range(heads):
        head = hg * heads + hh
        qn, kb, kn_t, kk_w, qk_w = qk_heads[hh // G_PAIR]
        pick = lambda a, off: jnp.sum(jnp.where(lane == head + off, a, 0.0), axis=-1, keepdims=True)
        beta_c = pick(bg, 0)
        gc_c = pick(gc_all, G_V)
        beta_r = rows_s[pl.ds(head, 1), :]
        gc_r = rows_s[pl.ds(3 * G_V + head, 1), :]
        gt_r = rows_s[pl.ds(5 * G_V + head, 1), :]
        per_head.append(dict(beta_r=beta_r, e_gc_r=jnp.exp(gc_r), e_gt_r=jnp.exp(gt_r),
                             e_rest_r=jnp.exp(gt_r - gc_r), qd=qn * jnp.exp(gc_c),
                             vb=v_ref[:, hh * G_DV:(hh + 1) * G_DV], kb=kb, kn_t=kn_t))
        for b in range(nblk):
            sl = slice(b * G_WIDE, (b + 1) * G_WIDE)
            decay_w = jnp.exp(jnp.where(incl_w, _col_to_wide(gc_c[sl], wblk) - gc_r[:, sl], NEG_BIG))
            n_w = jnp.where(strict_w, -(_col_to_wide(beta_c[sl], wblk) * kk_w[b] * decay_w), 0.0)
            qkm = jnp.where(incl_w, qk_w[b] * decay_w, 0.0).astype(BF16)
            chains.append(dict(hh=hh, sl=sl, n_w=n_w, qkm=qkm))

    for ch in chains:
        ch["t_w"] = eye_w + ch["n_w"]
        npow = ch["n_w"].astype(BF16)
        ch["nsq"] = _dot(npow, _block_diag(npow, wblk))
    for _ in range(4):
        for ch in chains:
            npow = ch["nsq"].astype(BF16)
            both = _dot(jnp.concatenate([npow, ch["t_w"].astype(BF16)], axis=0),
                        _block_diag(npow, wblk))
            ch["nsq"] = both[:G_CHUNK]
            ch["t_w"] = ch["t_w"] + both[G_CHUNK:]
    for ch in chains:
        ch["t_w"] = ch["t_w"] + _dot(ch["t_w"].astype(BF16),
                                     _block_diag(ch["nsq"].astype(BF16), wblk))

    for ch in chains:
        ph, sl = per_head[ch["hh"]], ch["sl"]
        t_beta = (ch["t_w"] * ph["beta_r"][:, sl]).astype(BF16)
        t_beta_e = (ch["t_w"] * (ph["beta_r"][:, sl] * ph["e_gc_r"][:, sl])).astype(BF16)
        u = _dot(_block_diag(t_beta, wblk), ph["vb"][sl])
        w = _dot(_block_diag(t_beta_e, wblk), ph["kb"][sl])
        ch["wu"] = jnp.concatenate([w, u], axis=1).astype(BF16)
    for ch in chains:
        ph, sl = per_head[ch["hh"]], ch["sl"]
        kd_t = (ph["kn_t"][:, sl] * ph["e_rest_r"][:, sl]).astype(BF16)
        zero = jnp.zeros_like(kd_t)
        lhs = jnp.concatenate([_block_diag(ch["qkm"], wblk)]
                              + [jnp.where(kblk == cc, kd_t, zero) for cc in range(G_CPB)], axis=0)
        big = _dot(lhs, ch["wu"])
        ch["qp"] = (ph["qd"][sl] - big[:G_WIDE, :G_DV]).astype(BF16)
        ch["qu"] = big[:G_WIDE, G_DV:]
        ch["wk"] = [big[G_WIDE + cc * G_DK:G_WIDE + (cc + 1) * G_DK, :G_DV].astype(BF16)
                    for cc in range(G_CPB)]
        ch["uk"] = [big[G_WIDE + cc * G_DK:G_WIDE + (cc + 1) * G_DK, G_DV:]
                    for cc in range(G_CPB)]

    states = [st_s[hh] for hh in range(heads)]
    outs = [[] for _ in range(heads)]
    for b in range(nblk):
        for cc in range(G_CPB):
            rows = slice(cc * G_CHUNK, (cc + 1) * G_CHUNK)
            t0 = b * G_WIDE + cc * G_CHUNK
            for hh in range(heads):
                ch = chains[hh * nblk + b]
                xs = _dot(jnp.concatenate([ch["wk"][cc], ch["qp"][rows]], axis=0),
                          states[hh].astype(BF16))
                outs[hh].append(xs[G_DK:] + ch["qu"][rows])
                states[hh] = (states[hh] * per_head[hh]["e_gt_r"][:, t0:t0 + 1]
                              - xs[:G_DK] + ch["uk"][cc])
    for hh in range(heads):
        st_s[hh] = states[hh]
        o = jnp.concatenate(outs[hh], axis=0)
        z = z_ref[:, hh * G_DV:(hh + 1) * G_DV].astype(F32)
        y = _rms(o, ngain) * _silu(z)
        out_ref[:, hh * G_DV:(hh + 1) * G_DV] = y.astype(BF16)


def _gdn_chunk(pa, bg, bgt, ngain, *, batch, seq, tc, heads):
    m = pa.shape[0]
    nc = seq // tc
    kern = functools.partial(_gdn_chunk_kernel, tc=tc, heads=heads)
    qk_w = heads // G_PAIR * G_DK
    v_w = heads * G_DV
    rowi = lambda b, g, c: b * nc + c
    return pl.pallas_call(
        kern,
        grid=(batch, G_V // heads, nc),
        in_specs=[
            pl.BlockSpec((tc, qk_w), lambda b, g, c: (rowi(b, g, c), g)),
            pl.BlockSpec((tc, qk_w), lambda b, g, c: (rowi(b, g, c), G_KEY // qk_w + g)),
            pl.BlockSpec((tc, v_w), lambda b, g, c: (rowi(b, g, c), 2 * G_KEY // v_w + g)),
            pl.BlockSpec((tc, v_w), lambda b, g, c: (rowi(b, g, c), G_QKV // v_w + g)),
            pl.BlockSpec((tc, LANES), lambda b, g, c: (rowi(b, g, c), 0)),
            pl.BlockSpec((None, 2 * G_V, tc), lambda b, g, c: (b, 0, c)),
            pl.BlockSpec((1, G_DV), lambda b, g, c: (0, 0)),
        ],
        out_specs=pl.BlockSpec((tc, v_w), lambda b, g, c: (rowi(b, g, c), g)),
        out_shape=jax.ShapeDtypeStruct((m, G_VAL), BF16),
        scratch_shapes=[pltpu.VMEM((heads, G_DK, G_DV), F32),
                        pltpu.VMEM((6 * G_V, tc), F32)],
        compiler_params=pltpu.CompilerParams(
            dimension_semantics=("arbitrary", "arbitrary", "arbitrary"),
            vmem_limit_bytes=VMEM_LIMIT),
        name="gdn_chunk",
    )(pa, pa, pa, pa, bg, bgt, ngain)


def _out_proj_kernel(a_ref, w_ref, h_ref, out_ref):
    out_ref[...] = h_ref[...] + _dot(a_ref[...], w_ref[...])


def _out_proj(a, w, h, *, tm):
    m, kdim = a.shape
    return pl.pallas_call(
        _out_proj_kernel,
        grid=(m // tm,),
        in_specs=[
            pl.BlockSpec((tm, kdim), lambda i: (i, 0)),
            pl.BlockSpec((kdim, D_MODEL), lambda i: (0, 0)),
            pl.BlockSpec((tm, D_MODEL), lambda i: (i, 0)),
        ],
        out_specs=pl.BlockSpec((tm, D_MODEL), lambda i: (i, 0)),
        out_shape=jax.ShapeDtypeStruct((m, D_MODEL), F32),
        compiler_params=pltpu.CompilerParams(dimension_semantics=("arbitrary",),
                                             vmem_limit_bytes=VMEM_LIMIT),
        name="out_proj",
    )(a, w, h)


def _rope_tables(positions):
    half = ROPE_DIM // 2
    inv_freq = ROPE_THETA ** (-jnp.arange(0, ROPE_DIM, 2, dtype=F32) / ROPE_DIM)
    ang = positions.astype(F32).reshape(-1, 1) * inv_freq
    cos, sin = jnp.cos(ang), jnp.sin(ang)
    n = ang.shape[0]
    cos_h = jnp.concatenate([cos, cos, jnp.ones((n, A_HD - 2 * half), F32)], axis=1)
    sin_h = jnp.concatenate([-sin, sin, jnp.zeros((n, A_HD - 2 * half), F32)], axis=1)
    return jnp.tile(cos_h, (1, LANES // A_HD)), jnp.tile(sin_h, (1, LANES // A_HD))


def kernel(x, positions, mixer_norm, ffn_norm, attn_w_in, attn_q_gain, attn_k_gain, attn_sinks,
           attn_w_out, gdn_w_in, gdn_conv, gdn_a_log, gdn_dt_bias, gdn_norm, gdn_w_out,
           ffn_w_up, ffn_conv, ffn_conv_b, ffn_w_down):
    batch, seq, _ = x.shape
    depth = mixer_norm.shape[0]
    m = batch * seq
    attn_tile = min(512, seq)
    row_tile = min(1024, seq)
    gdn_tc = min(512, seq)

    cosf, sins = _rope_tables(positions)
    h = x.reshape(m, D_MODEL)
    for i in range(depth):
        j = i // 2
        gain = mixer_norm[i].reshape(1, D_MODEL)
        if i % 2 == 0:
            qg = jnp.tile(attn_q_gain[j].reshape(1, A_HD), (1, LANES // A_HD))
            kg = jnp.tile(attn_k_gain[j].reshape(1, A_HD), (1, LANES // A_HD))
            h = _attn_layer(h, gain, attn_w_in[j].astype(BF16), attn_w_out[j].astype(BF16),
                            cosf, sins, qg, kg, attn_sinks[j].astype(F32),
                            seq=seq, tile=attn_tile)
        else:
            w_in = gdn_w_in[j]
            w_main = w_in[:, :G_MAIN].astype(BF16)
            w_ba = jnp.pad(w_in[:, G_MAIN:], ((0, 0), (0, LANES - 2 * G_V))).astype(BF16)
            pad16 = lambda a: jnp.pad(a.astype(F32).reshape(1, G_V), ((0, 0), (G_V, LANES - 2 * G_V)))
            pa, bg = _gdn_in_proj(h, gain, w_main, w_ba, gdn_conv[j], pad16(gdn_a_log[j]),
                                  pad16(gdn_dt_bias[j]), seq=seq, tm=row_tile, tn=1024)
            bgt = jnp.transpose(bg.reshape(batch, seq, LANES)[:, :, :2 * G_V], (0, 2, 1))
            o = _gdn_chunk(pa, bg, bgt, gdn_norm[j].reshape(1, G_DV), batch=batch, seq=seq,
                           tc=gdn_tc, heads=4)
            h = _out_proj(o, gdn_w_out[j].astype(BF16), h, tm=min(512, seq))
        h = _ffn_layer(h, ffn_norm[i].reshape(1, D_MODEL), ffn_w_up[i].astype(BF16), ffn_conv[i],
                       ffn_conv_b[i].reshape(1, D_FF), ffn_w_down[i].astype(BF16),
                       seq=seq, tm=row_tile, tf=512)
    return h.reshape(batch, seq, D_MODEL)
```

```python
import functools

import jax
import jax.numpy as jnp
from jax import lax
from jax.experimental import pallas as pl
from jax.experimental.pallas import tpu as pltpu

F32 = jnp.float32
BF16 = jnp.bfloat16
EPS = 1e-6

D_MODEL = 1024
A_HEADS = 16
A_KV = 4
A_GROUP = A_HEADS // A_KV
A_HD = 64
A_Q = A_HEADS * A_HD
A_KVD = A_KV * A_HD
A_BLK = 128
ROPE_DIM = A_HD // 4
ROPE_THETA = 500000.0
G_QK = 8
G_V = 16
G_DK = 128
G_DV = 128
G_CONV = 4
G_CHUNK = 64
G_KEY = G_QK * G_DK
G_VAL = G_V * G_DV
G_QKV = 2 * G_KEY + G_VAL
G_MAIN = G_QKV + G_VAL
D_FF = 3 * D_MODEL
F_CONV = 3

LANES = 128
SUBLANES = 8
VMEM_LIMIT = 56 * 1024 * 1024
NEG_BIG = -1e30


def _dot(a, b):
    return jnp.dot(a, b, preferred_element_type=F32)


def _dot_nt(a, b):
    return lax.dot_general(a, b, (((1,), (1,)), ((), ())), preferred_element_type=F32)


def _dot_tn(a, b):
    return lax.dot_general(a, b, (((0,), (0,)), ((), ())), preferred_element_type=F32)


def _rms(x, gain):
    ms = jnp.mean(x * x, axis=-1, keepdims=True)
    return x * lax.rsqrt(ms + EPS) * gain


def _silu(x):
    return x * (1.0 / (1.0 + jnp.exp(-x)))


def _split3(x):
    a = x.astype(BF16)
    r = x - a.astype(F32)
    b = r.astype(BF16)
    c = (r - b.astype(F32)).astype(BF16)
    return a, b, c


def _norm_rope_pair(x, gain, cosf, sins, lane):
    x2 = x * x
    lo = lane < A_HD
    s_lo = jnp.sum(jnp.where(lo, x2, 0.0), axis=-1, keepdims=True)
    s_hi = jnp.sum(jnp.where(lo, 0.0, x2), axis=-1, keepdims=True)
    ms = jnp.where(lo, s_lo, s_hi) * (1.0 / A_HD)
    y = x * lax.rsqrt(ms + EPS) * gain
    rot = jnp.where((lane % A_HD) < ROPE_DIM // 2,
                    pltpu.roll(y, LANES - ROPE_DIM // 2, 1), pltpu.roll(y, ROPE_DIM // 2, 1))
    return y * cosf + rot * sins


def _attn_kernel(sink_ref, h_ref, gain_ref, win_ref, wout_ref, cos_ref, sin_ref, qg_ref, kg_ref,
                 out_ref, q_s, klo_s, khi_s, vlo_s, vhi_s, o_s, *, tile, tiles_per_seq):
    step = pl.program_id(0)
    first = (step % tiles_per_seq) == 0
    nblk = tile // A_BLK
    kv_bufs = (klo_s, khi_s, vlo_s, vhi_s)

    @pl.when(first)
    def _():
        for buf in kv_bufs:
            buf[0:A_BLK, :] = jnp.zeros((A_BLK, A_KV * LANES), BF16)

    @pl.when(jnp.logical_not(first))
    def _():
        for buf in kv_bufs:
            buf[0:A_BLK, :] = buf[tile:tile + A_BLK, :]

    x = h_ref[...]
    xn = _rms(x, gain_ref[...]).astype(BF16)
    qkv = _dot(xn, win_ref[...])

    cosf = cos_ref[...]
    sins = sin_ref[...]
    lane = lax.broadcasted_iota(jnp.int32, (tile, LANES), 1)
    low = lane < A_HD
    qg = qg_ref[...]
    kg = kg_ref[...]
    for p in range(A_Q // LANES):
        c0 = p * LANES
        qp = _norm_rope_pair(qkv[:, c0:c0 + LANES], qg, cosf, sins, lane)
        q_s[:, c0:c0 + LANES] = (qp * (A_HD ** -0.5)).astype(BF16)

    def place(pair, lo_s, hi_s, p):
        swapped = pltpu.roll(pair, A_HD, 1)
        rows = slice(A_BLK, A_BLK + tile)
        g0, g1 = 2 * p, 2 * p + 1
        lo_s[rows, g0 * LANES:(g0 + 1) * LANES] = jnp.where(low, pair, 0.0).astype(BF16)
        hi_s[rows, g0 * LANES:(g0 + 1) * LANES] = jnp.where(low, 0.0, swapped).astype(BF16)
        lo_s[rows, g1 * LANES:(g1 + 1) * LANES] = jnp.where(low, swapped, 0.0).astype(BF16)
        hi_s[rows, g1 * LANES:(g1 + 1) * LANES] = jnp.where(low, 0.0, pair).astype(BF16)

    for p in range(A_KVD // LANES):
        c0 = A_Q + p * LANES
        place(_norm_rope_pair(qkv[:, c0:c0 + LANES], kg, cosf, sins, lane), klo_s, khi_s, p)
        place(qkv[:, c0 + A_KVD:c0 + A_KVD + LANES], vlo_s, vhi_s, p)

    rows2 = 2 * A_BLK
    qi = lax.broadcasted_iota(jnp.int32, (rows2, A_BLK), 0) % A_BLK
    kj = lax.broadcasted_iota(jnp.int32, (rows2, A_BLK), 1)
    causal = kj <= qi
    ones = jnp.ones((2 * A_BLK, LANES), BF16)

    for blk in range(nblk):
        r0 = blk * A_BLK
        has_prev = jnp.logical_not(first) if blk == 0 else None
        for g in range(A_KV):
            c0 = g * A_GROUP * A_HD
            qst = jnp.concatenate([q_s[r0:r0 + A_BLK, c0:c0 + LANES],
                                   q_s[r0:r0 + A_BLK, c0 + LANES:c0 + 2 * LANES]], axis=0)
            acc = None
            for par, (k_s, v_s) in enumerate(((klo_s, vlo_s), (khi_s, vhi_s))):
                kb = k_s[r0:r0 + 2 * A_BLK, g * LANES:(g + 1) * LANES]
                vb = v_s[r0:r0 + 2 * A_BLK, g * LANES:(g + 1) * LANES]
                s = _dot_nt(qst, kb)
                s_prev = s[:, :A_BLK]
                if has_prev is not None:
                    s_prev = jnp.where(has_prev, s_prev, NEG_BIG)
                sc = jnp.where(causal, s[:, A_BLK:], s_prev)
                sink = jnp.concatenate(
                    [jnp.full((A_BLK, 1), sink_ref[g * A_GROUP + 2 * pr + par], F32)
                     for pr in range(2)], axis=0)
                m = jnp.maximum(jnp.max(sc, axis=-1, keepdims=True), sink)
                p = jnp.exp(sc - m)
                pp = jnp.concatenate([jnp.where(causal, 0.0, p), jnp.where(causal, p, 0.0)],
                                     axis=1).astype(BF16)
                r = _dot(pp, jnp.concatenate([vb, ones], axis=1))
                o = r[:, :LANES] * (1.0 / (r[:, LANES:] + jnp.exp(sink - m)))
                acc = o if acc is None else acc + o
            o_s[r0:r0 + A_BLK, c0:c0 + LANES] = acc[:A_BLK].astype(BF16)
            o_s[r0:r0 + A_BLK, c0 + LANES:c0 + 2 * LANES] = acc[A_BLK:].astype(BF16)

    out_ref[...] = x + _dot(o_s[...], wout_ref[...])


def _attn_layer(h, gain, w_in, w_out, cosf, sins, qg, kg, sinks, *, seq, tile):
    m = h.shape[0]
    tiles_per_seq = seq // tile
    kern = functools.partial(_attn_kernel, tile=tile, tiles_per_seq=tiles_per_seq)
    row = lambda i: (i, 0)
    fixed = lambda i: (0, 0)
    return pl.pallas_call(
        kern,
        grid=(m // tile,),
        in_specs=[
            pl.BlockSpec(memory_space=pltpu.SMEM),
            pl.BlockSpec((tile, D_MODEL), row),
            pl.BlockSpec((1, D_MODEL), fixed),
            pl.BlockSpec((D_MODEL, A_Q + 2 * A_KVD), fixed),
            pl.BlockSpec((A_Q, D_MODEL), fixed),
            pl.BlockSpec((tile, LANES), row),
            pl.BlockSpec((tile, LANES), row),
            pl.BlockSpec((1, LANES), fixed),
            pl.BlockSpec((1, LANES), fixed),
        ],
        out_specs=pl.BlockSpec((tile, D_MODEL), row),
        out_shape=jax.ShapeDtypeStruct((m, D_MODEL), F32),
        scratch_shapes=[
            pltpu.VMEM((tile, A_Q), BF16),
            pltpu.VMEM((tile + A_BLK, A_KV * LANES), BF16),
            pltpu.VMEM((tile + A_BLK, A_KV * LANES), BF16),
            pltpu.VMEM((tile + A_BLK, A_KV * LANES), BF16),
            pltpu.VMEM((tile + A_BLK, A_KV * LANES), BF16),
            pltpu.VMEM((tile, A_Q), BF16),
        ],
        compiler_params=pltpu.CompilerParams(dimension_semantics=("arbitrary",),
                                             vmem_limit_bytes=VMEM_LIMIT),
        name="attn_layer",
    )(sinks, h, gain, w_in, w_out, cosf, sins, qg, kg)


CONV_ROWS = 256
LOOKAHEAD = 2


def _conv_stage(vals, ext, first):
    tm = vals.shape[0]
    ext[0:SUBLANES, :] = jnp.where(first, 0.0, ext[tm:tm + SUBLANES, :])
    ext[SUBLANES:SUBLANES + tm, :] = vals


def _conv_block(ext, r, cw, taps):
    xe = ext[r:r + SUBLANES + CONV_ROWS, :]
    z = xe * cw[0:1, :]
    for k in range(1, taps):
        z = pltpu.roll(z, 1, 0) + xe * cw[k:k + 1, :]
    return z[SUBLANES:]


def _resident(shape):
    return pl.BlockSpec(shape, lambda *_: (0,) * len(shape), pipeline_mode=pl.Buffered(1))


def _ffn_kernel(h_ref, gain_ref, wup_ref, cw_ref, cb_ref, wd_ref, out_ref, ext_s, up_s, act_s,
                *, tiles_per_seq, tf, down_every):
    i = pl.program_id(0)
    first = (i % tiles_per_seq) == 0
    tm = h_ref.shape[0]
    nj = D_FF // tf

    @pl.when(i == 0)
    def _():
        ext_s[:, tm:tm + SUBLANES, :] = jnp.zeros((nj, SUBLANES, tf), F32)

    x = h_ref[...]
    xn = _rms(x, gain_ref[...]).astype(BF16)

    def produce(j):
        _conv_stage(_dot(xn, wup_ref[:, j * tf:(j + 1) * tf]), ext_s.at[j], first)
        up_s[j] = _dot(xn, wup_ref[:, D_FF + j * tf:D_FF + (j + 1) * tf])

    acc = x
    for j in range(min(LOOKAHEAD, nj)):
        produce(j)
    for j in range(nj):
        if j + LOOKAHEAD < nj:
            produce(j + LOOKAHEAD)
        cols = slice(j * tf, (j + 1) * tf)
        cw = cw_ref[:, cols]
        cb = cb_ref[:, cols]
        for r in range(0, tm, CONV_ROWS):
            conv = _conv_block(ext_s.at[j], r, cw, F_CONV) + cb
            act_s[r:r + CONV_ROWS, cols] = (_silu(conv) * up_s[j, r:r + CONV_ROWS, :]).astype(BF16)
        if (j + 1) % down_every == 0:
            kc = slice((j + 1 - down_every) * tf, (j + 1) * tf)
            acc = acc + _dot(act_s[:, kc], wd_ref[kc, :])
    out_ref[...] = acc


def _ffn_layer(h, gain, w_up, conv_w, conv_b, w_down, *, seq, tm, tf):
    m = h.shape[0]
    nj = D_FF // tf
    kern = functools.partial(_ffn_kernel, tiles_per_seq=seq // tm, tf=tf,
                             down_every=D_MODEL // tf)
    return pl.pallas_call(
        kern,
        grid=(m // tm,),
        in_specs=[
            pl.BlockSpec((tm, D_MODEL), lambda i: (i, 0)),
            _resident((1, D_MODEL)),
            _resident((D_MODEL, 2 * D_FF)),
            _resident((F_CONV, D_FF)),
            _resident((1, D_FF)),
            _resident((D_FF, D_MODEL)),
        ],
        out_specs=pl.BlockSpec((tm, D_MODEL), lambda i: (i, 0)),
        out_shape=jax.ShapeDtypeStruct((m, D_MODEL), F32),
        scratch_shapes=[pltpu.VMEM((nj, tm + SUBLANES, tf), F32),
                        pltpu.VMEM((nj, tm, tf), F32),
                        pltpu.VMEM((tm, D_FF), BF16)],
        compiler_params=pltpu.CompilerParams(dimension_semantics=("arbitrary",),
                                             vmem_limit_bytes=VMEM_LIMIT),
        name="ffn_layer",
    )(h, gain, w_up, conv_w, conv_b, w_down)


def _gdn_in_kernel(h_ref, gain_ref, w_ref, wba_ref, cw_ref, alog_ref, dtb_ref, out_ref, bg_ref,
                   ext_s, *, tiles_per_seq, tn):
    i = pl.program_id(0)
    first = (i % tiles_per_seq) == 0
    tm = h_ref.shape[0]
    n_conv = G_QKV // tn
    n_all = G_MAIN // tn

    @pl.when(i == 0)
    def _():
        ext_s[:, tm:tm + SUBLANES, :] = jnp.zeros((n_conv, SUBLANES, tn), F32)

    xn = _rms(h_ref[...], gain_ref[...]).astype(BF16)
    ba = _dot(xn, wba_ref[...])
    lane = lax.broadcasted_iota(jnp.int32, ba.shape, 1)
    beta = 1.0 / (1.0 + jnp.exp(-ba))
    sp_in = ba + dtb_ref[...]
    softplus = jnp.maximum(sp_in, 0.0) + jnp.log1p(jnp.exp(-jnp.abs(sp_in)))
    g = -jnp.exp(alog_ref[...]) * softplus
    bg_ref[...] = jnp.where(lane < G_V, beta, jnp.where(lane < 2 * G_V, g, 0.0))

    def produce(j):
        cols = slice(j * tn, (j + 1) * tn)
        proj = _dot(xn, w_ref[:, cols])
        if j < n_conv:
            _conv_stage(proj, ext_s.at[j], first)
        else:
            out_ref[:, cols] = proj.astype(BF16)

    for j in range(min(LOOKAHEAD, n_all)):
        produce(j)
    for j in range(n_all):
        if j + LOOKAHEAD < n_all:
            produce(j + LOOKAHEAD)
        if j < n_conv:
            cols = slice(j * tn, (j + 1) * tn)
            cw = cw_ref[:, cols]
            for r in range(0, tm, CONV_ROWS):
                out_ref[r:r + CONV_ROWS, cols] = _silu(
                    _conv_block(ext_s.at[j], r, cw, G_CONV)).astype(BF16)


def _gdn_in_proj(h, gain, w_main, w_ba, conv_w, alog, dtb, *, seq, tm, tn):
    m = h.shape[0]
    kern = functools.partial(_gdn_in_kernel, tiles_per_seq=seq // tm, tn=tn)
    return pl.pallas_call(
        kern,
        grid=(m // tm,),
        in_specs=[
            pl.BlockSpec((tm, D_MODEL), lambda i: (i, 0)),
            _resident((1, D_MODEL)),
            _resident((D_MODEL, G_MAIN)),
            _resident((D_MODEL, LANES)),
            _resident((G_CONV, G_QKV)),
            _resident((1, LANES)),
            _resident((1, LANES)),
        ],
        out_specs=[
            pl.BlockSpec((tm, G_MAIN), lambda i: (i, 0)),
            pl.BlockSpec((tm, LANES), lambda i: (i, 0)),
        ],
        out_shape=[
            jax.ShapeDtypeStruct((m, G_MAIN), BF16),
            jax.ShapeDtypeStruct((m, LANES), F32),
        ],
        scratch_shapes=[pltpu.VMEM((G_QKV // tn, tm + SUBLANES, tn), F32)],
        compiler_params=pltpu.CompilerParams(dimension_semantics=("arbitrary",),
                                             vmem_limit_bytes=VMEM_LIMIT),
        name="gdn_in_proj",
    )(h, gain, w_main, w_ba, conv_w, alog, dtb)


G_PAIR = G_V // G_QK


G_WIDE = 256
G_CPB = G_WIDE // G_CHUNK


def _to_wide(full, blk):
    out = full[0:G_CHUNK]
    for cc in range(1, G_CPB):
        out = jnp.where(blk == cc, full[cc * G_CHUNK:(cc + 1) * G_CHUNK], out)
    return out


def _col_to_wide(col, blk):
    out = col[0:G_CHUNK]
    for cc in range(1, G_CPB):
        out = jnp.where(blk == cc, col[cc * G_CHUNK:(cc + 1) * G_CHUNK], out)
    return out


def _block_diag(wide, blk):
    zero = jnp.zeros_like(wide)
    return jnp.concatenate([jnp.where(blk == cc, wide, zero) for cc in range(G_CPB)], axis=0)


def _gdn_chunk_kernel(q_ref, k_ref, v_ref, z_ref, bg_ref, bgt_ref, ng_ref, out_ref, st_s, rows_s,
                      *, tc, heads):
    hg = pl.program_id(1)
    c = pl.program_id(2)
    nblk = tc // G_WIDE

    @pl.when(c == 0)
    def _():
        st_s[...] = jnp.zeros(st_s.shape, F32)

    ri = lax.broadcasted_iota(jnp.int32, (G_WIDE, G_WIDE), 0)
    ci = lax.broadcasted_iota(jnp.int32, (G_WIDE, G_WIDE), 1)
    same = (ri // G_CHUNK) == (ci // G_CHUNK)
    ones_incl = jnp.where(jnp.logical_and(same, ci <= ri), 1.0, 0.0).astype(BF16)
    ones_inclT = jnp.where(jnp.logical_and(same, ri <= ci), 1.0, 0.0).astype(BF16)
    ones_same = jnp.where(same, 1.0, 0.0).astype(BF16)

    wi = lax.broadcasted_iota(jnp.int32, (G_CHUNK, G_WIDE), 0)
    wl = lax.broadcasted_iota(jnp.int32, (G_CHUNK, G_WIDE), 1)
    wblk = wl // G_CHUNK
    wj = wl % G_CHUNK
    incl_w = wj <= wi
    strict_w = wj < wi
    eye_w = jnp.where(wj == wi, 1.0, 0.0)
    kblk = lax.broadcasted_iota(jnp.int32, (G_DK, G_WIDE), 1) // G_CHUNK

    bg = bg_ref[...]
    bgt = bgt_ref[...]
    gc_cols, gc_rows, gt_rows = [], [], []
    for b in range(nblk):
        sl = slice(b * G_WIDE, (b + 1) * G_WIDE)
        gcc = jnp.zeros((G_WIDE, LANES), F32)
        for piece in _split3(bg[sl]):
            gcc = gcc + _dot(ones_incl, piece)
        gc_cols.append(gcc)
        gcr = jnp.zeros((2 * G_V, G_WIDE), F32)
        gtr = jnp.zeros((2 * G_V, G_WIDE), F32)
        for piece in _split3(bgt[:, sl]):
            gcr = gcr + _dot(piece, ones_inclT)
            gtr = gtr + _dot(piece, ones_same)
        gc_rows.append(gcr)
        gt_rows.append(gtr)
    gc_all = jnp.concatenate(gc_cols, axis=0)
    rows_s[0:2 * G_V, :] = bgt
    rows_s[2 * G_V:4 * G_V, :] = jnp.concatenate(gc_rows, axis=1)
    rows_s[4 * G_V:6 * G_V, :] = jnp.concatenate(gt_rows, axis=1)

    lane = lax.broadcasted_iota(jnp.int32, (tc, LANES), 1)
    ngain = ng_ref[...]

    qk_heads = []
    for qh in range(heads // G_PAIR):
        q = q_ref[:, qh * G_DK:(qh + 1) * G_DK].astype(F32)
        k = k_ref[:, qh * G_DK:(qh + 1) * G_DK].astype(F32)
        qn = q * lax.rsqrt(jnp.sum(q * q, axis=-1, keepdims=True) + EPS) * (G_DK ** -0.5)
        kn = k * lax.rsqrt(jnp.sum(k * k, axis=-1, keepdims=True) + EPS)
        qb = qn.astype(BF16)
        kb = kn.astype(BF16)
        kn_t = kn.T
        kk_w, qk_w = [], []
        for b in range(nblk):
            sl = slice(b * G_WIDE, (b + 1) * G_WIDE)
            kk_w.append(_to_wide(_dot_nt(kb[sl], kb[sl]), wblk))
            qk_w.append(_to_wide(_dot_nt(qb[sl], kb[sl]), wblk))
        qk_heads.append((qn, kb, kn_t, kk_w, qk_w))

    per_head = []
    chains = []
    for hh in range(heads):
        head = hg * heads + hh
        qn, kb, kn_t, kk_w, qk_w = qk_heads[hh // G_PAIR]
        pick = lambda a, off: jnp.sum(jnp.where(lane == head + off, a, 0.0), axis=-1, keepdims=True)
        beta_c = pick(bg, 0)
        gc_c = pick(gc_all, G_V)
        beta_r = rows_s[pl.ds(head, 1), :]
        gc_r = rows_s[pl.ds(3 * G_V + head, 1), :]
        gt_r = rows_s[pl.ds(5 * G_V + head, 1), :]
        per_head.append(dict(beta_r=beta_r, e_gc_r=jnp.exp(gc_r), e_gt_r=jnp.exp(gt_r),
                             e_rest_r=jnp.exp(gt_r - gc_r), qd=qn * jnp.exp(gc_c),
                             vb=v_ref[:, hh * G_DV:(hh + 1) * G_DV], kb=kb, kn_t=kn_t))
        for b in range(nblk):
            sl = slice(b * G_WIDE, (b + 1) * G_WIDE)
            decay_w = jnp.exp(jnp.where(incl_w, _col_to_wide(gc_c[sl], wblk) - gc_r[:, sl], NEG_BIG))
            n_w = jnp.where(strict_w, -(_col_to_wide(beta_c[sl], wblk) * kk_w[b] * decay_w), 0.0)
            qkm = jnp.where(incl_w, qk_w[b] * decay_w, 0.0).astype(BF16)
            chains.append(dict(hh=hh, sl=sl, n_w=n_w, qkm=qkm))

    for ch in chains:
        ch["t_w"] = eye_w + ch["n_w"]
        npow = ch["n_w"].astype(BF16)
        ch["nsq"] = _dot(npow, _block_diag(npow, wblk))
    for _ in range(4):
        for ch in chains:
            npow = ch["nsq"].astype(BF16)
            both = _dot(jnp.concatenate([npow, ch["t_w"].astype(BF16)], axis=0),
                        _block_diag(npow, wblk))
            ch["nsq"] = both[:G_CHUNK]
            ch["t_w"] = ch["t_w"] + both[G_CHUNK:]
    for ch in chains:
        ch["t_w"] = ch["t_w"] + _dot(ch["t_w"].astype(BF16),
                                     _block_diag(ch["nsq"].astype(BF16), wblk))

    for ch in chains:
        ph, sl = per_head[ch["hh"]], ch["sl"]
        t_beta = (ch["t_w"] * ph["beta_r"][:, sl]).astype(BF16)
        t_beta_e = (ch["t_w"] * (ph["beta_r"][:, sl] * ph["e_gc_r"][:, sl])).astype(BF16)
        u = _dot(_block_diag(t_beta, wblk), ph["vb"][sl])
        w = _dot(_block_diag(t_beta_e, wblk), ph["kb"][sl])
        ch["wu"] = jnp.concatenate([w, u], axis=1).astype(BF16)
    for ch in chains:
        ph, sl = per_head[ch["hh"]], ch["sl"]
        kd_t = (ph["kn_t"][:, sl] * ph["e_rest_r"][:, sl]).astype(BF16)
        zero = jnp.zeros_like(kd_t)
        lhs = jnp.concatenate([_block_diag(ch["qkm"], wblk)]
                              + [jnp.where(kblk == cc, kd_t, zero) for cc in range(G_CPB)], axis=0)
        big = _dot(lhs, ch["wu"])
        ch["qp"] = (ph["qd"][sl] - big[:G_WIDE, :G_DV]).astype(BF16)
        ch["qu"] = big[:G_WIDE, G_DV:]
        ch["wk"] = [big[G_WIDE + cc * G_DK:G_WIDE + (cc + 1) * G_DK, :G_DV].astype(BF16)
                    for cc in range(G_CPB)]
        ch["uk"] = [big[G_WIDE + cc * G_DK:G_WIDE + (cc + 1) * G_DK, G_DV:]
                    for cc in range(G_CPB)]

    states = [st_s[hh] for hh in range(heads)]
    outs = [[] for _ in range(heads)]
    for b in range(nblk):
        for cc in range(G_CPB):
            rows = slice(cc * G_CHUNK, (cc + 1) * G_CHUNK)
            t0 = b * G_WIDE + cc * G_CHUNK
            for hh in range(heads):
                ch = chains[hh * nblk + b]
                xs = _dot(jnp.concatenate([ch["wk"][cc], ch["qp"][rows]], axis=0),
                          states[hh].astype(BF16))
                outs[hh].append(xs[G_DK:] + ch["qu"][rows])
                states[hh] = (states[hh] * per_head[hh]["e_gt_r"][:, t0:t0 + 1]
                              - xs[:G_DK] + ch["uk"][cc])
    for hh in range(heads):
        st_s[hh] = states[hh]
        o = jnp.concatenate(outs[hh], axis=0)
        z = z_ref[:, hh * G_DV:(hh + 1) * G_DV].astype(F32)
        y = _rms(o, ngain) * _silu(z)
        out_ref[:, hh * G_DV:(hh + 1) * G_DV] = y.astype(BF16)


def _gdn_chunk(pa, bg, bgt, ngain, *, batch, seq, tc, heads):
    m = pa.shape[0]
    nc = seq // tc
    kern = functools.partial(_gdn_chunk_kernel, tc=tc, heads=heads)
    qk_w = heads // G_PAIR * G_DK
    v_w = heads * G_DV
    rowi = lambda b, g, c: b * nc + c
    return pl.pallas_call(
        kern,
        grid=(batch, G_V // heads, nc),
        in_specs=[
            pl.BlockSpec((tc, qk_w), lambda b, g, c: (rowi(b, g, c), g)),
            pl.BlockSpec((tc, qk_w), lambda b, g, c: (rowi(b, g, c), G_KEY // qk_w + g)),
            pl.BlockSpec((tc, v_w), lambda b, g, c: (rowi(b, g, c), 2 * G_KEY // v_w + g)),
            pl.BlockSpec((tc, v_w), lambda b, g, c: (rowi(b, g, c), G_QKV // v_w + g)),
            pl.BlockSpec((tc, LANES), lambda b, g, c: (rowi(b, g, c), 0)),
            pl.BlockSpec((None, 2 * G_V, tc), lambda b, g, c: (b, 0, c)),
            pl.BlockSpec((1, G_DV), lambda b, g, c: (0, 0)),
        ],
        out_specs=pl.BlockSpec((tc, v_w), lambda b, g, c: (rowi(b, g, c), g)),
        out_shape=jax.ShapeDtypeStruct((m, G_VAL), BF16),
        scratch_shapes=[pltpu.VMEM((heads, G_DK, G_DV), F32),
                        pltpu.VMEM((6 * G_V, tc), F32)],
        compiler_params=pltpu.CompilerParams(
            dimension_semantics=("arbitrary", "arbitrary", "arbitrary"),
            vmem_limit_bytes=VMEM_LIMIT),
        name="gdn_chunk",
    )(pa, pa, pa, pa, bg, bgt, ngain)


def _out_proj_kernel(a_ref, w_ref, h_ref, out_ref):
    out_ref[...] = h_ref[...] + _dot(a_ref[...], w_ref[...])


def _out_proj(a, w, h, *, tm):
    m, kdim = a.shape
    return pl.pallas_call(
        _out_proj_kernel,
        grid=(m // tm,),
        in_specs=[
            pl.BlockSpec((tm, kdim), lambda i: (i, 0)),
            pl.BlockSpec((kdim, D_MODEL), lambda i: (0, 0)),
            pl.BlockSpec((tm, D_MODEL), lambda i: (i, 0)),
        ],
        out_specs=pl.BlockSpec((tm, D_MODEL), lambda i: (i, 0)),
        out_shape=jax.ShapeDtypeStruct((m, D_MODEL), F32),
        compiler_params=pltpu.CompilerParams(dimension_semantics=("arbitrary",),
                                             vmem_limit_bytes=VMEM_LIMIT),
        name="out_proj",
    )(a, w, h)


def _rope_tables(positions):
    half = ROPE_DIM // 2
    inv_freq = ROPE_THETA ** (-jnp.arange(0, ROPE_DIM, 2, dtype=F32) / ROPE_DIM)
    ang = positions.astype(F32).reshape(-1, 1) * inv_freq
    cos, sin = jnp.cos(ang), jnp.sin(ang)
    n = ang.shape[0]
    cos_h = jnp.concatenate([cos, cos, jnp.ones((n, A_HD - 2 * half), F32)], axis=1)
    sin_h = jnp.concatenate([-sin, sin, jnp.zeros((n, A_HD - 2 * half), F32)], axis=1)
    return jnp.tile(cos_h, (1, LANES // A_HD)), jnp.tile(sin_h, (1, LANES // A_HD))


def kernel(x, positions, mixer_norm, ffn_norm, attn_w_in, attn_q_gain, attn_k_gain, attn_sinks,
           attn_w_out, gdn_w_in, gdn_conv, gdn_a_log, gdn_dt_bias, gdn_norm, gdn_w_out,
           ffn_w_up, ffn_conv, ffn_conv_b, ffn_w_down):
    batch, seq, _ = x.shape
    depth = mixer_norm.shape[0]
    m = batch * seq
    attn_tile = min(512, seq)
    row_tile = min(512, seq)
    gdn_tc = min(512, seq)

    cosf, sins = _rope_tables(positions)
    h = x.reshape(m, D_MODEL)
    for i in range(depth):
        j = i // 2
        gain = mixer_norm[i].reshape(1, D_MODEL)
        if i % 2 == 0:
            qg = jnp.tile(attn_q_gain[j].reshape(1, A_HD), (1, LANES // A_HD))
            kg = jnp.tile(attn_k_gain[j].reshape(1, A_HD), (1, LANES // A_HD))
            h = _attn_layer(h, gain, attn_w_in[j].astype(BF16), attn_w_out[j].astype(BF16),
                            cosf, sins, qg, kg, attn_sinks[j].astype(F32),
                            seq=seq, tile=attn_tile)
        else:
            w_in = gdn_w_in[j]
            w_main = w_in[:, :G_MAIN].astype(BF16)
            w_ba = jnp.pad(w_in[:, G_MAIN:], ((0, 0), (0, LANES - 2 * G_V))).astype(BF16)
            pad16 = lambda a: jnp.pad(a.astype(F32).reshape(1, G_V), ((0, 0), (G_V, LANES - 2 * G_V)))
            pa, bg = _gdn_in_proj(h, gain, w_main, w_ba, gdn_conv[j], pad16(gdn_a_log[j]),
                                  pad16(gdn_dt_bias[j]), seq=seq, tm=min(256, seq), tn=256)
            bgt = jnp.transpose(bg.reshape(batch, seq, LANES)[:, :, :2 * G_V], (0, 2, 1))
            o = _gdn_chunk(pa, bg, bgt, gdn_norm[j].reshape(1, G_DV), batch=batch, seq=seq,
                           tc=gdn_tc, heads=4)
            h = _out_proj(o, gdn_w_out[j].astype(BF16), h, tm=min(512, seq))
        h = _ffn_layer(h, ffn_norm[i].reshape(1, D_MODEL), ffn_w_up[i].astype(BF16), ffn_conv[i],
                       ffn_conv_b[i].reshape(1, D_FF), ffn_w_down[i].astype(BF16),
                       seq=seq, tm=row_tile, tf=256)
    return h.reshape(batch, seq, D_MODEL)
```

```python
import functools

import jax
import jax.numpy as jnp
from jax import lax
from jax.experimental import pallas as pl
from jax.experimental.pallas import tpu as pltpu

F32 = jnp.float32
BF16 = jnp.bfloat16
EPS = 1e-6

D_MODEL = 1024
A_HEADS = 16
A_KV = 4
A_GROUP = A_HEADS // A_KV
A_HD = 64
A_Q = A_HEADS * A_HD
A_KVD = A_KV * A_HD
A_BLK = 128
ROPE_DIM = A_HD // 4
ROPE_THETA = 500000.0
G_QK = 8
G_V = 16
G_DK = 128
G_DV = 128
G_CONV = 4
G_CHUNK = 64
G_KEY = G_QK * G_DK
G_VAL = G_V * G_DV
G_QKV = 2 * G_KEY + G_VAL
G_MAIN = G_QKV + G_VAL
D_FF = 3 * D_MODEL
F_CONV = 3

LANES = 128
SUBLANES = 8
VMEM_LIMIT = 56 * 1024 * 1024
NEG_BIG = -1e30


def _dot(a, b):
    return jnp.dot(a, b, preferred_element_type=F32)


def _dot_nt(a, b):
    return lax.dot_general(a, b, (((1,), (1,)), ((), ())), preferred_element_type=F32)


def _dot_tn(a, b):
    return lax.dot_general(a, b, (((0,), (0,)), ((), ())), preferred_element_type=F32)


def _rms(x, gain):
    ms = jnp.mean(x * x, axis=-1, keepdims=True)
    return x * lax.rsqrt(ms + EPS) * gain


def _silu(x):
    hx = 0.5 * x
    return hx * (1.0 + jnp.tanh(hx))


def _split3(x):
    a = x.astype(BF16)
    r = x - a.astype(F32)
    b = r.astype(BF16)
    c = (r - b.astype(F32)).astype(BF16)
    return a, b, c


def _norm_rope_pair(x, gain, cosf, sins, lane):
    x2 = x * x
    lo = lane < A_HD
    s_lo = jnp.sum(jnp.where(lo, x2, 0.0), axis=-1, keepdims=True)
    s_hi = jnp.sum(jnp.where(lo, 0.0, x2), axis=-1, keepdims=True)
    ms = jnp.where(lo, s_lo, s_hi) * (1.0 / A_HD)
    y = x * lax.rsqrt(ms + EPS) * gain
    rot = jnp.where((lane % A_HD) < ROPE_DIM // 2,
                    pltpu.roll(y, LANES - ROPE_DIM // 2, 1), pltpu.roll(y, ROPE_DIM // 2, 1))
    return y * cosf + rot * sins


def _attn_kernel(sink_ref, h_ref, gain_ref, win_ref, wout_ref, cos_ref, sin_ref, qg_ref, kg_ref,
                 out_ref, q_s, klo_s, khi_s, vlo_s, vhi_s, o_s, *, tile, tiles_per_seq):
    step = pl.program_id(0)
    first = (step % tiles_per_seq) == 0
    nblk = tile // A_BLK
    kv_bufs = (klo_s, khi_s, vlo_s, vhi_s)

    @pl.when(first)
    def _():
        for buf in kv_bufs:
            buf[0:A_BLK, :] = jnp.zeros((A_BLK, A_KV * LANES), BF16)

    @pl.when(jnp.logical_not(first))
    def _():
        for buf in kv_bufs:
            buf[0:A_BLK, :] = buf[tile:tile + A_BLK, :]

    x = h_ref[...]
    xn = _rms(x, gain_ref[...]).astype(BF16)
    qkv = _dot(xn, win_ref[...])

    cosf = cos_ref[...]
    sins = sin_ref[...]
    lane = lax.broadcasted_iota(jnp.int32, (tile, LANES), 1)
    low = lane < A_HD
    qg = qg_ref[...]
    kg = kg_ref[...]
    for p in range(A_Q // LANES):
        c0 = p * LANES
        qp = _norm_rope_pair(qkv[:, c0:c0 + LANES], qg, cosf, sins, lane)
        q_s[:, c0:c0 + LANES] = (qp * (A_HD ** -0.5)).astype(BF16)

    def place(pair, lo_s, hi_s, p):
        swapped = pltpu.roll(pair, A_HD, 1)
        rows = slice(A_BLK, A_BLK + tile)
        g0, g1 = 2 * p, 2 * p + 1
        lo_s[rows, g0 * LANES:(g0 + 1) * LANES] = jnp.where(low, pair, 0.0).astype(BF16)
        hi_s[rows, g0 * LANES:(g0 + 1) * LANES] = jnp.where(low, 0.0, swapped).astype(BF16)
        lo_s[rows, g1 * LANES:(g1 + 1) * LANES] = jnp.where(low, swapped, 0.0).astype(BF16)
        hi_s[rows, g1 * LANES:(g1 + 1) * LANES] = jnp.where(low, 0.0, pair).astype(BF16)

    for p in range(A_KVD // LANES):
        c0 = A_Q + p * LANES
        place(_norm_rope_pair(qkv[:, c0:c0 + LANES], kg, cosf, sins, lane), klo_s, khi_s, p)
        place(qkv[:, c0 + A_KVD:c0 + A_KVD + LANES], vlo_s, vhi_s, p)

    rows2 = 2 * A_BLK
    qi = lax.broadcasted_iota(jnp.int32, (rows2, A_BLK), 0) % A_BLK
    kj = lax.broadcasted_iota(jnp.int32, (rows2, A_BLK), 1)
    causal = kj <= qi
    ones = jnp.ones((2 * A_BLK, LANES), BF16)

    for blk in range(nblk):
        r0 = blk * A_BLK
        has_prev = jnp.logical_not(first) if blk == 0 else None
        for g in range(A_KV):
            c0 = g * A_GROUP * A_HD
            qst = jnp.concatenate([q_s[r0:r0 + A_BLK, c0:c0 + LANES],
                                   q_s[r0:r0 + A_BLK, c0 + LANES:c0 + 2 * LANES]], axis=0)
            acc = None
            for par, (k_s, v_s) in enumerate(((klo_s, vlo_s), (khi_s, vhi_s))):
                kb = k_s[r0:r0 + 2 * A_BLK, g * LANES:(g + 1) * LANES]
                vb = v_s[r0:r0 + 2 * A_BLK, g * LANES:(g + 1) * LANES]
                s = _dot_nt(qst, kb)
                s_prev = s[:, :A_BLK]
                if has_prev is not None:
                    s_prev = jnp.where(has_prev, s_prev, NEG_BIG)
                sc = jnp.where(causal, s[:, A_BLK:], s_prev)
                sink = jnp.concatenate(
                    [jnp.full((A_BLK, 1), sink_ref[g * A_GROUP + 2 * pr + par], F32)
                     for pr in range(2)], axis=0)
                m = jnp.maximum(jnp.max(sc, axis=-1, keepdims=True), sink)
                p = jnp.exp(sc - m)
                pp = jnp.concatenate([jnp.where(causal, 0.0, p), jnp.where(causal, p, 0.0)],
                                     axis=1).astype(BF16)
                r = _dot(pp, jnp.concatenate([vb, ones], axis=1))
                o = r[:, :LANES] * (1.0 / (r[:, LANES:] + jnp.exp(sink - m)))
                acc = o if acc is None else acc + o
            o_s[r0:r0 + A_BLK, c0:c0 + LANES] = acc[:A_BLK].astype(BF16)
            o_s[r0:r0 + A_BLK, c0 + LANES:c0 + 2 * LANES] = acc[A_BLK:].astype(BF16)

    out_ref[...] = x + _dot(o_s[...], wout_ref[...])


def _attn_layer(h, gain, w_in, w_out, cosf, sins, qg, kg, sinks, *, layer, seq, tile):
    m = h.shape[0]
    tiles_per_seq = seq // tile
    kern = functools.partial(_attn_kernel, tile=tile, tiles_per_seq=tiles_per_seq)
    row = lambda i: (i, 0)
    fixed = lambda i: (0, 0)
    return pl.pallas_call(
        kern,
        grid=(m // tile,),
        in_specs=[
            pl.BlockSpec(memory_space=pltpu.SMEM),
            pl.BlockSpec((tile, D_MODEL), row),
            pl.BlockSpec((1, D_MODEL), fixed),
            _resident((D_MODEL, A_Q + 2 * A_KVD), layer),
            _resident((A_Q, D_MODEL), layer),
            pl.BlockSpec((tile, LANES), row),
            pl.BlockSpec((tile, LANES), row),
            pl.BlockSpec((1, LANES), fixed),
            pl.BlockSpec((1, LANES), fixed),
        ],
        out_specs=pl.BlockSpec((tile, D_MODEL), row),
        out_shape=jax.ShapeDtypeStruct((m, D_MODEL), F32),
        scratch_shapes=[
            pltpu.VMEM((tile, A_Q), BF16),
            pltpu.VMEM((tile + A_BLK, A_KV * LANES), BF16),
            pltpu.VMEM((tile + A_BLK, A_KV * LANES), BF16),
            pltpu.VMEM((tile + A_BLK, A_KV * LANES), BF16),
            pltpu.VMEM((tile + A_BLK, A_KV * LANES), BF16),
            pltpu.VMEM((tile, A_Q), BF16),
        ],
        compiler_params=pltpu.CompilerParams(dimension_semantics=("arbitrary",),
                                             vmem_limit_bytes=VMEM_LIMIT),
        name="attn_layer",
    )(sinks, h, gain, w_in, w_out, cosf, sins, qg, kg)


CONV_ROWS = 256
LOOKAHEAD = 2


def _conv_stage(vals, ext, first):
    tm = vals.shape[0]
    ext[0:SUBLANES, :] = jnp.where(first, 0.0, ext[tm:tm + SUBLANES, :])
    ext[SUBLANES:SUBLANES + tm, :] = vals


def _conv_block(ext, r, cw, taps):
    xe = ext[r:r + SUBLANES + CONV_ROWS, :]
    if taps == 4:
        xs = pltpu.roll(xe, 1, 0)
        older = xs * cw[0:1, :] + xe * cw[1:2, :]
        newer = xs * cw[2:3, :] + xe * cw[3:4, :]
        return (pltpu.roll(older, 2, 0) + newer)[SUBLANES:]
    z = xe * cw[0:1, :]
    for k in range(1, taps):
        z = pltpu.roll(z, 1, 0) + xe * cw[k:k + 1, :]
    return z[SUBLANES:]


def _resident(shape, layer=None):
    if layer is None:
        return pl.BlockSpec(shape, lambda *_: (0,) * len(shape), pipeline_mode=pl.Buffered(1))
    return pl.BlockSpec((None,) + tuple(shape), lambda *_: (layer,) + (0,) * len(shape),
                        pipeline_mode=pl.Buffered(1))


def _ffn_kernel(*refs, tiles_per_seq, tf, down_every, mixer_out):
    if mixer_out:
        mix_ref, wmix_ref, *refs = refs
    h_ref, gain_ref, wup_ref, cw_ref, cb_ref, wd_ref, out_ref, ext_s, up_s, act_s = refs
    i = pl.program_id(0)
    first = (i % tiles_per_seq) == 0
    tm = h_ref.shape[0]
    nj = D_FF // tf

    @pl.when(i == 0)
    def _():
        ext_s[:, tm:tm + SUBLANES, :] = jnp.zeros((nj, SUBLANES, tf), F32)

    x = h_ref[...]
    if mixer_out:
        x = x + _dot(mix_ref[...], wmix_ref[...])
    xn = _rms(x, gain_ref[...]).astype(BF16)

    def produce(j):
        _conv_stage(_dot(xn, wup_ref[:, j * tf:(j + 1) * tf]), ext_s.at[j], first)
        up_s[j] = _dot(xn, wup_ref[:, D_FF + j * tf:D_FF + (j + 1) * tf])

    acc = x
    for j in range(min(LOOKAHEAD, nj)):
        produce(j)
    for j in range(nj):
        if j + LOOKAHEAD < nj:
            produce(j + LOOKAHEAD)
        cols = slice(j * tf, (j + 1) * tf)
        cw = cw_ref[:, cols]
        cb = cb_ref[:, cols]
        for r in range(0, tm, CONV_ROWS):
            conv = _conv_block(ext_s.at[j], r, cw, F_CONV) + cb
            act_s[r:r + CONV_ROWS, cols] = (_silu(conv) * up_s[j, r:r + CONV_ROWS, :]).astype(BF16)
        if (j + 1) % down_every == 0:
            kc = slice((j + 1 - down_every) * tf, (j + 1) * tf)
            acc = acc + _dot(act_s[:, kc], wd_ref[kc, :])
    out_ref[...] = acc


def _ffn_layer(h, gain, w_up, conv_w, conv_b, w_down, *, layer, seq, tm, tf, mixer=None):
    m = h.shape[0]
    nj = D_FF // tf
    kern = functools.partial(_ffn_kernel, tiles_per_seq=seq // tm, tf=tf,
                             down_every=D_MODEL // tf, mixer_out=mixer is not None)
    mixer_specs, mixer_args = [], []
    if mixer is not None:
        mix, w_mix, mix_layer = mixer
        kdim = mix.shape[1]
        mixer_specs = [pl.BlockSpec((tm, kdim), lambda i: (i, 0)),
                       _resident((kdim, D_MODEL), mix_layer)]
        mixer_args = [mix, w_mix]
    return pl.pallas_call(
        kern,
        grid=(m // tm,),
        in_specs=mixer_specs + [
            pl.BlockSpec((tm, D_MODEL), lambda i: (i, 0)),
            _resident((1, D_MODEL)),
            _resident((D_MODEL, 2 * D_FF), layer),
            _resident((F_CONV, D_FF)),
            _resident((1, D_FF)),
            _resident((D_FF, D_MODEL), layer),
        ],
        out_specs=pl.BlockSpec((tm, D_MODEL), lambda i: (i, 0)),
        out_shape=jax.ShapeDtypeStruct((m, D_MODEL), F32),
        scratch_shapes=[pltpu.VMEM((nj, tm + SUBLANES, tf), F32),
                        pltpu.VMEM((nj, tm, tf), F32),
                        pltpu.VMEM((tm, D_FF), BF16)],
        compiler_params=pltpu.CompilerParams(dimension_semantics=("arbitrary",),
                                             vmem_limit_bytes=VMEM_LIMIT),
        name="ffn_layer",
    )(*mixer_args, h, gain, w_up, conv_w, conv_b, w_down)


def _gdn_in_kernel(h_ref, gain_ref, w_ref, wba_ref, cw_ref, alog_ref, dtb_ref,
                   qkv_ref, z_ref, bg_ref, ext_s, *, tiles_per_seq, tn):
    i = pl.program_id(0)
    first = (i % tiles_per_seq) == 0
    tm = h_ref.shape[0]
    n_conv = G_QKV // tn
    n_all = G_MAIN // tn

    @pl.when(i == 0)
    def _():
        ext_s[:, tm:tm + SUBLANES, :] = jnp.zeros((n_conv, SUBLANES, tn), F32)

    xn = _rms(h_ref[...], gain_ref[...]).astype(BF16)
    ba = _dot(xn, wba_ref[...])
    lane = lax.broadcasted_iota(jnp.int32, ba.shape, 1)
    beta = 1.0 / (1.0 + jnp.exp(-ba))
    sp_in = ba + dtb_ref[...]
    softplus = jnp.maximum(sp_in, 0.0) + jnp.log1p(jnp.exp(-jnp.abs(sp_in)))
    g = -jnp.exp(alog_ref[...]) * softplus
    bg_ref[...] = jnp.where(lane < G_V, beta, jnp.where(lane < 2 * G_V, g, 0.0))

    def produce(j):
        cols = slice(j * tn, (j + 1) * tn)
        proj = _dot(xn, w_ref[:, cols])
        if j < n_conv:
            _conv_stage(proj, ext_s.at[j], first)
        else:
            z_ref[:, (j - n_conv) * tn:(j - n_conv + 1) * tn] = proj.astype(BF16)

    for j in range(min(LOOKAHEAD, n_all)):
        produce(j)
    for j in range(n_all):
        if j + LOOKAHEAD < n_all:
            produce(j + LOOKAHEAD)
        if j < n_conv:
            cols = slice(j * tn, (j + 1) * tn)
            cw = cw_ref[:, cols]
            for r in range(0, tm, CONV_ROWS):
                qkv_ref[r:r + CONV_ROWS, cols] = _silu(
                    _conv_block(ext_s.at[j], r, cw, G_CONV)).astype(BF16)


def _gdn_in_proj(h, gain, w_main, w_ba, conv_w, alog, dtb, *, layer, seq, tm, tn):
    m = h.shape[0]
    kern = functools.partial(_gdn_in_kernel, tiles_per_seq=seq // tm, tn=tn)
    cur_tile = lambda i: (i, 0)
    prev_tile = cur_tile
    return pl.pallas_call(
        kern,
        grid=(m // tm,),
        in_specs=[
            pl.BlockSpec((tm, D_MODEL), cur_tile),
            _resident((1, D_MODEL)),
            _resident((D_MODEL, G_MAIN), layer),
            _resident((D_MODEL, LANES)),
            _resident((G_CONV, G_QKV)),
            _resident((1, LANES)),
            _resident((1, LANES)),
        ],
        out_specs=[
            pl.BlockSpec((tm, G_QKV), prev_tile),
            pl.BlockSpec((tm, G_VAL), cur_tile),
            pl.BlockSpec((tm, LANES), cur_tile),
        ],
        out_shape=[
            jax.ShapeDtypeStruct((m, G_QKV), BF16),
            jax.ShapeDtypeStruct((m, G_VAL), BF16),
            jax.ShapeDtypeStruct((m, LANES), F32),
        ],
        scratch_shapes=[pltpu.VMEM((G_QKV // tn, tm + SUBLANES, tn), F32)],
        compiler_params=pltpu.CompilerParams(dimension_semantics=("arbitrary",),
                                             vmem_limit_bytes=VMEM_LIMIT),
        name="gdn_in_proj",
    )(h, gain, w_main, w_ba, conv_w, alog, dtb)


G_PAIR = G_V // G_QK


G_WIDE = 256
G_CPB = G_WIDE // G_CHUNK


def _to_wide(full, blk):
    out = full[0:G_CHUNK]
    for cc in range(1, G_CPB):
        out = jnp.where(blk == cc, full[cc * G_CHUNK:(cc + 1) * G_CHUNK], out)
    return out


def _col_to_wide(col, blk):
    out = col[0:G_CHUNK]
    for cc in range(1, G_CPB):
        out = jnp.where(blk == cc, col[cc * G_CHUNK:(cc + 1) * G_CHUNK], out)
    return out


def _block_diag(wide, blk):
    zero = jnp.zeros_like(wide)
    return jnp.concatenate([jnp.where(blk == cc, wide, zero) for cc in range(G_CPB)], axis=0)


def _gdn_chunk_kernel(q_ref, k_ref, v_ref, z_ref, bg_ref, bgt_ref, ng_ref, out_ref, st_s, rows_s,
                      *, tc, heads):
    hg = pl.program_id(1)
    c = pl.program_id(2)
    nblk = tc // G_WIDE

    @pl.when(c == 0)
    def _():
        st_s[...] = jnp.zeros(st_s.shape, F32)

    ri = lax.broadcasted_iota(jnp.int32, (G_WIDE, G_WIDE), 0)
    ci = lax.broadcasted_iota(jnp.int32, (G_WIDE, G_WIDE), 1)
    same = (ri // G_CHUNK) == (ci // G_CHUNK)
    ones_incl = jnp.where(jnp.logical_and(same, ci <= ri), 1.0, 0.0).astype(BF16)
    ones_inclT = jnp.where(jnp.logical_and(same, ri <= ci), 1.0, 0.0).astype(BF16)
    ones_same = jnp.where(same, 1.0, 0.0).astype(BF16)

    wi = lax.broadcasted_iota(jnp.int32, (G_CHUNK, G_WIDE), 0)
    wl = lax.broadcasted_iota(jnp.int32, (G_CHUNK, G_WIDE), 1)
    wblk = wl // G_CHUNK
    wj = wl % G_CHUNK
    incl_w = wj <= wi
    strict_w = wj < wi
    eye_w = jnp.where(wj == wi, 1.0, 0.0)
    kblk = lax.broadcasted_iota(jnp.int32, (G_DK, G_WIDE), 1) // G_CHUNK

    bg = bg_ref[...]
    bgt = bgt_ref[...]
    gc_cols, gc_rows, gt_rows = [], [], []
    for b in range(nblk):
        sl = slice(b * G_WIDE, (b + 1) * G_WIDE)
        gcc = jnp.zeros((G_WIDE, LANES), F32)
        for piece in _split3(bg[sl]):
            gcc = gcc + _dot(ones_incl, piece)
        gc_cols.append(gcc)
        gcr = jnp.zeros((2 * G_V, G_WIDE), F32)
        gtr = jnp.zeros((2 * G_V, G_WIDE), F32)
        for piece in _split3(bgt[:, sl]):
            gcr = gcr + _dot(piece, ones_inclT)
            gtr = gtr + _dot(piece, ones_same)
        gc_rows.append(gcr)
        gt_rows.append(gtr)
    gc_all = jnp.concatenate(gc_cols, axis=0)
    rows_s[0:2 * G_V, :] = bgt
    rows_s[2 * G_V:4 * G_V, :] = jnp.concatenate(gc_rows, axis=1)
    rows_s[4 * G_V:6 * G_V, :] = jnp.concatenate(gt_rows, axis=1)

    lane = lax.broadcasted_iota(jnp.int32, (tc, LANES), 1)
    ngain = ng_ref[...]

    qk_heads = []
    for qh in range(heads // G_PAIR):
        q = q_ref[:, qh * G_DK:(qh + 1) * G_DK].astype(F32)
        k = k_ref[:, qh * G_DK:(qh + 1) * G_DK].astype(F32)
        qn = q * lax.rsqrt(jnp.sum(q * q, axis=-1, keepdims=True) + EPS) * (G_DK ** -0.5)
        kn = k * lax.rsqrt(jnp.sum(k * k, axis=-1, keepdims=True) + EPS)
        qb = qn.astype(BF16)
        kb = kn.astype(BF16)
        kn_t = kn.T
        kk_w, qk_w = [], []
        for b in range(nblk):
            sl = slice(b * G_WIDE, (b + 1) * G_WIDE)
            kk_w.append(_to_wide(_dot_nt(kb[sl], kb[sl]), wblk))
            qk_w.append(_to_wide(_dot_nt(qb[sl], kb[sl]), wblk))
        qk_heads.append((qn, kb, kn_t, kk_w, qk_w))

    per_head = []
    chains = []
    for hh in range(heads):
        head = hg * heads + hh
        qn, kb, kn_t, kk_w, qk_w = qk_heads[hh // G_PAIR]
        pick = lambda a, off: jnp.sum(jnp.where(lane == head + off, a, 0.0), axis=-1, keepdims=True)
        beta_c = pick(bg, 0)
        gc_c = pick(gc_all, G_V)
        beta_r = rows_s[pl.ds(head, 1), :]
        gc_r = rows_s[pl.ds(3 * G_V + head, 1), :]
        gt_r = rows_s[pl.ds(5 * G_V + head, 1), :]
        per_head.append(dict(beta_r=beta_r, e_gc_r=jnp.exp(gc_r), e_gt_r=jnp.exp(gt_r),
                             e_rest_r=jnp.exp(gt_r - gc_r), qd=qn * jnp.exp(gc_c),
                             vb=v_ref[:, hh * G_DV:(hh + 1) * G_DV], kb=kb, kn_t=kn_t))
        for b in range(nblk):
            sl = slice(b * G_WIDE, (b + 1) * G_WIDE)
            decay_w = jnp.exp(jnp.where(incl_w, _col_to_wide(gc_c[sl], wblk) - gc_r[:, sl], NEG_BIG))
            n_w = jnp.where(strict_w, -(_col_to_wide(beta_c[sl], wblk) * kk_w[b] * decay_w), 0.0)
            qkm = jnp.where(incl_w, qk_w[b] * decay_w, 0.0).astype(BF16)
            chains.append(dict(hh=hh, sl=sl, n_w=n_w, qkm=qkm))

    for ch in chains:
        ch["t_w"] = eye_w + ch["n_w"]
        npow = ch["n_w"].astype(BF16)
        ch["nsq"] = _dot(npow, _block_diag(npow, wblk))
    for _ in range(4):
        for ch in chains:
            npow = ch["nsq"].astype(BF16)
            both = _dot(jnp.concatenate([npow, ch["t_w"].astype(BF16)], axis=0),
                        _block_diag(npow, wblk))
            ch["nsq"] = both[:G_CHUNK]
            ch["t_w"] = ch["t_w"] + both[G_CHUNK:]
    for ch in chains:
        ch["t_w"] = ch["t_w"] + _dot(ch["t_w"].astype(BF16),
                                     _block_diag(ch["nsq"].astype(BF16), wblk))

    for ch in chains:
        ph, sl = per_head[ch["hh"]], ch["sl"]
        t_beta = (ch["t_w"] * ph["beta_r"][:, sl]).astype(BF16)
        t_beta_e = (ch["t_w"] * (ph["beta_r"][:, sl] * ph["e_gc_r"][:, sl])).astype(BF16)
        u = _dot(_block_diag(t_beta, wblk), ph["vb"][sl])
        w = _dot(_block_diag(t_beta_e, wblk), ph["kb"][sl])
        ch["wu"] = jnp.concatenate([w, u], axis=1).astype(BF16)
    for ch in chains:
        ph, sl = per_head[ch["hh"]], ch["sl"]
        kd_t = (ph["kn_t"][:, sl] * ph["e_rest_r"][:, sl]).astype(BF16)
        zero = jnp.zeros_like(kd_t)
        lhs = jnp.concatenate([_block_diag(ch["qkm"], wblk)]
                              + [jnp.where(kblk == cc, kd_t, zero) for cc in range(G_CPB)], axis=0)
        big = _dot(lhs, ch["wu"])
        ch["qp"] = (ph["qd"][sl] - big[:G_WIDE, :G_DV]).astype(BF16)
        ch["qu"] = big[:G_WIDE, G_DV:]
        ch["wk"] = [big[G_WIDE + cc * G_DK:G_WIDE + (cc + 1) * G_DK, :G_DV].astype(BF16)
                    for cc in range(G_CPB)]
        ch["uk"] = [big[G_WIDE + cc * G_DK:G_WIDE + (cc + 1) * G_DK, G_DV:]
                    for cc in range(G_CPB)]

    states = [st_s[hh] for hh in range(heads)]
    outs = [[] for _ in range(heads)]
    for b in range(nblk):
        for cc in range(G_CPB):
            rows = slice(cc * G_CHUNK, (cc + 1) * G_CHUNK)
            t0 = b * G_WIDE + cc * G_CHUNK
            for hh in range(heads):
                ch = chains[hh * nblk + b]
                xs = _dot(jnp.concatenate([ch["wk"][cc], ch["qp"][rows]], axis=0),
                          states[hh].astype(BF16))
                outs[hh].append(xs[G_DK:] + ch["qu"][rows])
                states[hh] = (states[hh] * per_head[hh]["e_gt_r"][:, t0:t0 + 1]
                              - xs[:G_DK] + ch["uk"][cc])
    for hh in range(heads):
        st_s[hh] = states[hh]
        o = jnp.concatenate(outs[hh], axis=0)
        z = z_ref[:, hh * G_DV:(hh + 1) * G_DV].astype(F32)
        y = _rms(o, ngain) * _silu(z)
        out_ref[:, hh * G_DV:(hh + 1) * G_DV] = y.astype(BF16)


def _gdn_chunk(qkv, z, bg, bgt, ngain, *, batch, seq, tc, heads):
    m = qkv.shape[0]
    nc = seq // tc
    kern = functools.partial(_gdn_chunk_kernel, tc=tc, heads=heads)
    qk_w = heads // G_PAIR * G_DK
    v_w = heads * G_DV
    rowi = lambda b, g, c: b * nc + c
    return pl.pallas_call(
        kern,
        grid=(batch, G_V // heads, nc),
        in_specs=[
            pl.BlockSpec((tc, qk_w), lambda b, g, c: (rowi(b, g, c), g)),
            pl.BlockSpec((tc, qk_w), lambda b, g, c: (rowi(b, g, c), G_KEY // qk_w + g)),
            pl.BlockSpec((tc, v_w), lambda b, g, c: (rowi(b, g, c), 2 * G_KEY // v_w + g)),
            pl.BlockSpec((tc, v_w), lambda b, g, c: (rowi(b, g, c), g)),
            pl.BlockSpec((tc, LANES), lambda b, g, c: (rowi(b, g, c), 0)),
            pl.BlockSpec((None, 2 * G_V, tc), lambda b, g, c: (b, 0, c)),
            pl.BlockSpec((1, G_DV), lambda b, g, c: (0, 0)),
        ],
        out_specs=pl.BlockSpec((tc, v_w), lambda b, g, c: (rowi(b, g, c), g)),
        out_shape=jax.ShapeDtypeStruct((m, G_VAL), BF16),
        scratch_shapes=[pltpu.VMEM((heads, G_DK, G_DV), F32),
                        pltpu.VMEM((6 * G_V, tc), F32)],
        compiler_params=pltpu.CompilerParams(
            dimension_semantics=("arbitrary", "arbitrary", "arbitrary"),
            vmem_limit_bytes=VMEM_LIMIT),
        name="gdn_chunk",
    )(qkv, qkv, qkv, z, bg, bgt, ngain)


def _rope_tables(positions):
    half = ROPE_DIM // 2
    inv_freq = ROPE_THETA ** (-jnp.arange(0, ROPE_DIM, 2, dtype=F32) / ROPE_DIM)
    ang = positions.astype(F32).reshape(-1, 1) * inv_freq
    cos, sin = jnp.cos(ang), jnp.sin(ang)
    n = ang.shape[0]
    cos_h = jnp.concatenate([cos, cos, jnp.ones((n, A_HD - 2 * half), F32)], axis=1)
    sin_h = jnp.concatenate([-sin, sin, jnp.zeros((n, A_HD - 2 * half), F32)], axis=1)
    return jnp.tile(cos_h, (1, LANES // A_HD)), jnp.tile(sin_h, (1, LANES // A_HD))


def kernel(x, positions, mixer_norm, ffn_norm, attn_w_in, attn_q_gain, attn_k_gain, attn_sinks,
           attn_w_out, gdn_w_in, gdn_conv, gdn_a_log, gdn_dt_bias, gdn_norm, gdn_w_out,
           ffn_w_up, ffn_conv, ffn_conv_b, ffn_w_down):
    batch, seq, _ = x.shape
    depth = mixer_norm.shape[0]
    m = batch * seq
    attn_tile = min(512, seq)
    row_tile = min(512, seq)
    gdn_tc = min(512, seq)

    cosf, sins = _rope_tables(positions)
    attn_w_in_b = attn_w_in.astype(BF16)
    attn_w_out_b = attn_w_out.astype(BF16)
    gdn_w_main_b = gdn_w_in[:, :, :G_MAIN].astype(BF16)
    gdn_w_out_b = gdn_w_out.astype(BF16)
    ffn_w_up_b = ffn_w_up.astype(BF16)
    ffn_w_down_b = ffn_w_down.astype(BF16)

    h = x.reshape(m, D_MODEL)
    for i in range(depth):
        j = i // 2
        gain = mixer_norm[i].reshape(1, D_MODEL)
        mixer = None
        if i % 2 == 0:
            qg = jnp.tile(attn_q_gain[j].reshape(1, A_HD), (1, LANES // A_HD))
            kg = jnp.tile(attn_k_gain[j].reshape(1, A_HD), (1, LANES // A_HD))
            h = _attn_layer(h, gain, attn_w_in_b, attn_w_out_b, cosf, sins, qg, kg,
                            attn_sinks[j].astype(F32), layer=j, seq=seq, tile=attn_tile)
        else:
            w_ba = jnp.pad(gdn_w_in[j][:, G_MAIN:], ((0, 0), (0, LANES - 2 * G_V))).astype(BF16)
            pad16 = lambda a: jnp.pad(a.astype(F32).reshape(1, G_V), ((0, 0), (G_V, LANES - 2 * G_V)))
            qkv, z, bg = _gdn_in_proj(h, gain, gdn_w_main_b, w_ba, gdn_conv[j],
                                      pad16(gdn_a_log[j]), pad16(gdn_dt_bias[j]),
                                      layer=j, seq=seq, tm=row_tile, tn=256)
            bgt = jnp.transpose(bg.reshape(batch, seq, LANES)[:, :, :2 * G_V], (0, 2, 1))
            o = _gdn_chunk(qkv, z, bg, bgt, gdn_norm[j].reshape(1, G_DV), batch=batch, seq=seq,
                           tc=gdn_tc, heads=8)
            mixer = (o, gdn_w_out_b, j)
        h = _ffn_layer(h, ffn_norm[i].reshape(1, D_MODEL), ffn_w_up_b, ffn_conv[i],
                       ffn_conv_b[i].reshape(1, D_FF), ffn_w_down_b,
                       layer=i, seq=seq, tm=row_tile, tf=256, mixer=mixer)
    return h.reshape(batch, seq, D_MODEL)
```

```python
import functools

import jax
import jax.numpy as jnp
from jax import lax
from jax.experimental import pallas as pl
from jax.experimental.pallas import tpu as pltpu

F32 = jnp.float32
BF16 = jnp.bfloat16
EPS = 1e-6

D_MODEL = 1024
A_HEADS = 16
A_KV = 4
A_GROUP = A_HEADS // A_KV
A_HD = 64
A_Q = A_HEADS * A_HD
A_KVD = A_KV * A_HD
A_BLK = 128
ROPE_DIM = A_HD // 4
ROPE_THETA = 500000.0
G_QK = 8
G_V = 16
G_DK = 128
G_DV = 128
G_CONV = 4
G_CHUNK = 64
G_KEY = G_QK * G_DK
G_VAL = G_V * G_DV
G_QKV = 2 * G_KEY + G_VAL
G_MAIN = G_QKV + G_VAL
D_FF = 3 * D_MODEL
F_CONV = 3

LANES = 128
SUBLANES = 8
VMEM_LIMIT = 56 * 1024 * 1024
NEG_BIG = -1e30


def _dot(a, b):
    return jnp.dot(a, b, preferred_element_type=F32)


def _dot_nt(a, b):
    return lax.dot_general(a, b, (((1,), (1,)), ((), ())), preferred_element_type=F32)


def _dot_tn(a, b):
    return lax.dot_general(a, b, (((0,), (0,)), ((), ())), preferred_element_type=F32)


def _rms(x, gain):
    ms = jnp.mean(x * x, axis=-1, keepdims=True)
    return x * lax.rsqrt(ms + EPS) * gain


def _silu(x):
    hx = 0.5 * x
    return hx * (1.0 + jnp.tanh(hx))


def _split3(x):
    a = x.astype(BF16)
    r = x - a.astype(F32)
    b = r.astype(BF16)
    c = (r - b.astype(F32)).astype(BF16)
    return a, b, c


def _norm_rope_pair(x, gain, cosf, sins, lane):
    x2 = x * x
    lo = lane < A_HD
    s_lo = jnp.sum(jnp.where(lo, x2, 0.0), axis=-1, keepdims=True)
    s_hi = jnp.sum(jnp.where(lo, 0.0, x2), axis=-1, keepdims=True)
    ms = jnp.where(lo, s_lo, s_hi) * (1.0 / A_HD)
    y = x * lax.rsqrt(ms + EPS) * gain
    rot = jnp.where((lane % A_HD) < ROPE_DIM // 2,
                    pltpu.roll(y, LANES - ROPE_DIM // 2, 1), pltpu.roll(y, ROPE_DIM // 2, 1))
    return y * cosf + rot * sins


def _attn_kernel(sink_ref, h_ref, gain_ref, win_ref, wout_ref, cos_ref, sin_ref, qg_ref, kg_ref,
                 out_ref, q_s, klo_s, khi_s, vlo_s, vhi_s, o_s, *, tile, tiles_per_seq):
    step = pl.program_id(0)
    first = (step % tiles_per_seq) == 0
    nblk = tile // A_BLK
    kv_bufs = (klo_s, khi_s, vlo_s, vhi_s)

    @pl.when(first)
    def _():
        for buf in kv_bufs:
            buf[0:A_BLK, :] = jnp.zeros((A_BLK, A_KV * LANES), BF16)

    @pl.when(jnp.logical_not(first))
    def _():
        for buf in kv_bufs:
            buf[0:A_BLK, :] = buf[tile:tile + A_BLK, :]

    x = h_ref[...]
    xn = _rms(x, gain_ref[...]).astype(BF16)
    qkv = _dot(xn, win_ref[...])

    cosf = cos_ref[...]
    sins = sin_ref[...]
    lane = lax.broadcasted_iota(jnp.int32, (tile, LANES), 1)
    low = lane < A_HD
    qg = qg_ref[...]
    kg = kg_ref[...]
    for p in range(A_Q // LANES):
        c0 = p * LANES
        qp = _norm_rope_pair(qkv[:, c0:c0 + LANES], qg, cosf, sins, lane)
        q_s[:, c0:c0 + LANES] = qp.astype(BF16)

    def place(pair, lo_s, hi_s, p):
        swapped = pltpu.roll(pair, A_HD, 1)
        rows = slice(A_BLK, A_BLK + tile)
        g0, g1 = 2 * p, 2 * p + 1
        lo_s[rows, g0 * LANES:(g0 + 1) * LANES] = jnp.where(low, pair, 0.0).astype(BF16)
        hi_s[rows, g0 * LANES:(g0 + 1) * LANES] = jnp.where(low, 0.0, swapped).astype(BF16)
        lo_s[rows, g1 * LANES:(g1 + 1) * LANES] = jnp.where(low, swapped, 0.0).astype(BF16)
        hi_s[rows, g1 * LANES:(g1 + 1) * LANES] = jnp.where(low, 0.0, pair).astype(BF16)

    for p in range(A_KVD // LANES):
        c0 = A_Q + p * LANES
        place(_norm_rope_pair(qkv[:, c0:c0 + LANES], kg, cosf, sins, lane), klo_s, khi_s, p)
        place(qkv[:, c0 + A_KVD:c0 + A_KVD + LANES], vlo_s, vhi_s, p)

    rows2 = 2 * A_BLK
    qi = lax.broadcasted_iota(jnp.int32, (rows2, A_BLK), 0) % A_BLK
    kj = lax.broadcasted_iota(jnp.int32, (rows2, A_BLK), 1)
    causal = kj <= qi
    ones = jnp.ones((2 * A_BLK, LANES), BF16)

    for blk in range(nblk):
        r0 = blk * A_BLK
        has_prev = jnp.logical_not(first) if blk == 0 else None
        for g in range(A_KV):
            c0 = g * A_GROUP * A_HD
            qst = jnp.concatenate([q_s[r0:r0 + A_BLK, c0:c0 + LANES],
                                   q_s[r0:r0 + A_BLK, c0 + LANES:c0 + 2 * LANES]], axis=0)
            acc = None
            for par, (k_s, v_s) in enumerate(((klo_s, vlo_s), (khi_s, vhi_s))):
                kb = k_s[r0:r0 + 2 * A_BLK, g * LANES:(g + 1) * LANES]
                vb = v_s[r0:r0 + 2 * A_BLK, g * LANES:(g + 1) * LANES]
                s = _dot_nt(qst, kb)
                s_prev = s[:, :A_BLK]
                if has_prev is not None:
                    s_prev = jnp.where(has_prev, s_prev, NEG_BIG)
                sc = jnp.where(causal, s[:, A_BLK:], s_prev)
                sink = jnp.concatenate(
                    [jnp.full((A_BLK, 1), sink_ref[g * A_GROUP + 2 * pr + par], F32)
                     for pr in range(2)], axis=0)
                m = jnp.maximum(jnp.max(sc, axis=-1, keepdims=True), sink)
                p = jnp.exp(sc - m)
                pp = jnp.concatenate([jnp.where(causal, 0.0, p), jnp.where(causal, p, 0.0)],
                                     axis=1).astype(BF16)
                r = _dot(pp, jnp.concatenate([vb, ones], axis=1))
                o = r[:, :LANES] * (1.0 / (r[:, LANES:] + jnp.exp(sink - m)))
                acc = o if acc is None else acc + o
            o_s[r0:r0 + A_BLK, c0:c0 + LANES] = acc[:A_BLK].astype(BF16)
            o_s[r0:r0 + A_BLK, c0 + LANES:c0 + 2 * LANES] = acc[A_BLK:].astype(BF16)

    out_ref[...] = x + _dot(o_s[...], wout_ref[...])


def _attn_layer(h, gain, w_in, w_out, cosf, sins, qg, kg, sinks, *, layer, seq, tile):
    m = h.shape[0]
    tiles_per_seq = seq // tile
    kern = functools.partial(_attn_kernel, tile=tile, tiles_per_seq=tiles_per_seq)
    row = lambda i: (i, 0)
    fixed = lambda i: (0, 0)
    return pl.pallas_call(
        kern,
        grid=(m // tile,),
        in_specs=[
            pl.BlockSpec(memory_space=pltpu.SMEM),
            pl.BlockSpec((tile, D_MODEL), row),
            pl.BlockSpec((1, D_MODEL), fixed),
            _resident((D_MODEL, A_Q + 2 * A_KVD), layer),
            _resident((A_Q, D_MODEL), layer),
            pl.BlockSpec((tile, LANES), row),
            pl.BlockSpec((tile, LANES), row),
            pl.BlockSpec((1, LANES), fixed),
            pl.BlockSpec((1, LANES), fixed),
        ],
        out_specs=pl.BlockSpec((tile, D_MODEL), row),
        out_shape=jax.ShapeDtypeStruct((m, D_MODEL), F32),
        scratch_shapes=[
            pltpu.VMEM((tile, A_Q), BF16),
            pltpu.VMEM((tile + A_BLK, A_KV * LANES), BF16),
            pltpu.VMEM((tile + A_BLK, A_KV * LANES), BF16),
            pltpu.VMEM((tile + A_BLK, A_KV * LANES), BF16),
            pltpu.VMEM((tile + A_BLK, A_KV * LANES), BF16),
            pltpu.VMEM((tile, A_Q), BF16),
        ],
        compiler_params=pltpu.CompilerParams(dimension_semantics=("arbitrary",),
                                             vmem_limit_bytes=VMEM_LIMIT),
        name="attn_layer",
    )(sinks, h, gain, w_in, w_out, cosf, sins, qg, kg)


CONV_ROWS = 256
LOOKAHEAD = 2


def _conv_stage(vals, ext, first):
    tm = vals.shape[0]
    ext[0:SUBLANES, :] = jnp.where(first, 0.0, ext[tm:tm + SUBLANES, :])
    ext[SUBLANES:SUBLANES + tm, :] = vals


def _conv_block(ext, r, cw, taps):
    xe = ext[r:r + SUBLANES + CONV_ROWS, :]
    if taps == 4:
        xs = pltpu.roll(xe, 1, 0)
        older = xs * cw[0:1, :] + xe * cw[1:2, :]
        newer = xs * cw[2:3, :] + xe * cw[3:4, :]
        return (pltpu.roll(older, 2, 0) + newer)[SUBLANES:]
    z = xe * cw[0:1, :]
    for k in range(1, taps):
        z = pltpu.roll(z, 1, 0) + xe * cw[k:k + 1, :]
    return z[SUBLANES:]


def _resident(shape, layer=None):
    if layer is None:
        return pl.BlockSpec(shape, lambda *_: (0,) * len(shape), pipeline_mode=pl.Buffered(1))
    return pl.BlockSpec((None,) + tuple(shape), lambda *_: (layer,) + (0,) * len(shape),
                        pipeline_mode=pl.Buffered(1))


def _ffn_kernel(*refs, tiles_per_seq, tf, down_every, mixer_out):
    if mixer_out:
        mix_ref, wmix_ref, *refs = refs
    h_ref, gain_ref, wup_ref, cw_ref, cb_ref, wd_ref, out_ref, ext_s, up_s, act_s = refs
    i = pl.program_id(0)
    first = (i % tiles_per_seq) == 0
    tm = h_ref.shape[0]
    nj = D_FF // tf

    @pl.when(i == 0)
    def _():
        ext_s[:, tm:tm + SUBLANES, :] = jnp.zeros((nj, SUBLANES, tf), F32)

    x = h_ref[...]
    if mixer_out:
        x = x + _dot(mix_ref[...], wmix_ref[...])
    xn = _rms(x, gain_ref[...]).astype(BF16)

    def produce(j):
        _conv_stage(_dot(xn, wup_ref[:, j * tf:(j + 1) * tf]), ext_s.at[j], first)
        up_s[j] = _dot(xn, wup_ref[:, D_FF + j * tf:D_FF + (j + 1) * tf])

    acc = x
    for j in range(min(LOOKAHEAD, nj)):
        produce(j)
    for j in range(nj):
        if j + LOOKAHEAD < nj:
            produce(j + LOOKAHEAD)
        cols = slice(j * tf, (j + 1) * tf)
        cw = cw_ref[:, cols]
        cb = cb_ref[:, cols]
        for r in range(0, tm, CONV_ROWS):
            conv = _conv_block(ext_s.at[j], r, cw, F_CONV) + cb
            act_s[r:r + CONV_ROWS, cols] = (_silu(conv) * up_s[j, r:r + CONV_ROWS, :]).astype(BF16)
        if (j + 1) % down_every == 0:
            kc = slice((j + 1 - down_every) * tf, (j + 1) * tf)
            acc = acc + _dot(act_s[:, kc], wd_ref[kc, :])
    out_ref[...] = acc


def _ffn_layer(h, gain, w_up, conv_w, conv_b, w_down, *, layer, seq, tm, tf, mixer=None):
    m = h.shape[0]
    nj = D_FF // tf
    kern = functools.partial(_ffn_kernel, tiles_per_seq=seq // tm, tf=tf,
                             down_every=D_MODEL // tf, mixer_out=mixer is not None)
    mixer_specs, mixer_args = [], []
    if mixer is not None:
        mix, w_mix, mix_layer = mixer
        kdim = mix.shape[1]
        mixer_specs = [pl.BlockSpec((tm, kdim), lambda i: (i, 0)),
                       _resident((kdim, D_MODEL), mix_layer)]
        mixer_args = [mix, w_mix]
    return pl.pallas_call(
        kern,
        grid=(m // tm,),
        in_specs=mixer_specs + [
            pl.BlockSpec((tm, D_MODEL), lambda i: (i, 0)),
            _resident((1, D_MODEL)),
            _resident((D_MODEL, 2 * D_FF), layer),
            _resident((F_CONV, D_FF)),
            _resident((1, D_FF)),
            _resident((D_FF, D_MODEL), layer),
        ],
        out_specs=pl.BlockSpec((tm, D_MODEL), lambda i: (i, 0)),
        out_shape=jax.ShapeDtypeStruct((m, D_MODEL), F32),
        scratch_shapes=[pltpu.VMEM((nj, tm + SUBLANES, tf), F32),
                        pltpu.VMEM((nj, tm, tf), F32),
                        pltpu.VMEM((tm, D_FF), BF16)],
        compiler_params=pltpu.CompilerParams(dimension_semantics=("arbitrary",),
                                             vmem_limit_bytes=VMEM_LIMIT),
        name="ffn_layer",
    )(*mixer_args, h, gain, w_up, conv_w, conv_b, w_down)


def _gdn_in_kernel(h_ref, gain_ref, w_ref, wba_ref, cw_ref, alog_ref, dtb_ref,
                   qkv_ref, z_ref, bg_ref, ext_s, *, tiles_per_seq, tn):
    i = pl.program_id(0)
    first = (i % tiles_per_seq) == 0
    tm = h_ref.shape[0]
    n_conv = G_QKV // tn
    n_all = G_MAIN // tn

    @pl.when(i == 0)
    def _():
        ext_s[:, tm:tm + SUBLANES, :] = jnp.zeros((n_conv, SUBLANES, tn), F32)

    xn = _rms(h_ref[...], gain_ref[...]).astype(BF16)
    ba = _dot(xn, wba_ref[...])
    lane = lax.broadcasted_iota(jnp.int32, ba.shape, 1)
    beta = 1.0 / (1.0 + jnp.exp(-ba))
    sp_in = ba + dtb_ref[...]
    softplus = jnp.maximum(sp_in, 0.0) + jnp.log1p(jnp.exp(-jnp.abs(sp_in)))
    g = -jnp.exp(alog_ref[...]) * softplus
    bg_ref[...] = jnp.where(lane < G_V, beta, jnp.where(lane < 2 * G_V, g, 0.0))

    def produce(j):
        cols = slice(j * tn, (j + 1) * tn)
        proj = _dot(xn, w_ref[:, cols])
        if j < n_conv:
            _conv_stage(proj, ext_s.at[j], first)
        else:
            z_ref[:, (j - n_conv) * tn:(j - n_conv + 1) * tn] = proj.astype(BF16)

    n_z = n_all - n_conv
    per_z = n_conv // n_z
    order = []
    for zc in range(n_z):
        order += list(range(zc * per_z, (zc + 1) * per_z)) + [n_conv + zc]
    for j in order[:LOOKAHEAD]:
        produce(j)
    for pos, j in enumerate(order):
        if pos + LOOKAHEAD < n_all:
            produce(order[pos + LOOKAHEAD])
        if j < n_conv:
            cols = slice(j * tn, (j + 1) * tn)
            cw = cw_ref[:, cols]
            for r in range(0, tm, CONV_ROWS):
                qkv_ref[r:r + CONV_ROWS, cols] = _silu(
                    _conv_block(ext_s.at[j], r, cw, G_CONV)).astype(BF16)


def _gdn_in_proj(h, gain, w_main, w_ba, conv_w, alog, dtb, *, layer, seq, tm, tn):
    m = h.shape[0]
    kern = functools.partial(_gdn_in_kernel, tiles_per_seq=seq // tm, tn=tn)
    cur_tile = lambda i: (i, 0)
    prev_tile = cur_tile
    return pl.pallas_call(
        kern,
        grid=(m // tm,),
        in_specs=[
            pl.BlockSpec((tm, D_MODEL), cur_tile),
            _resident((1, D_MODEL)),
            _resident((D_MODEL, G_MAIN), layer),
            _resident((D_MODEL, LANES)),
            _resident((G_CONV, G_QKV)),
            _resident((1, LANES)),
            _resident((1, LANES)),
        ],
        out_specs=[
            pl.BlockSpec((tm, G_QKV), prev_tile),
            pl.BlockSpec((tm, G_VAL), cur_tile),
            pl.BlockSpec((tm, LANES), cur_tile),
        ],
        out_shape=[
            jax.ShapeDtypeStruct((m, G_QKV), BF16),
            jax.ShapeDtypeStruct((m, G_VAL), BF16),
            jax.ShapeDtypeStruct((m, LANES), F32),
        ],
        scratch_shapes=[pltpu.VMEM((G_QKV // tn, tm + SUBLANES, tn), F32)],
        compiler_params=pltpu.CompilerParams(dimension_semantics=("arbitrary",),
                                             vmem_limit_bytes=VMEM_LIMIT),
        name="gdn_in_proj",
    )(h, gain, w_main, w_ba, conv_w, alog, dtb)


G_PAIR = G_V // G_QK


G_WIDE = 256
G_CPB = G_WIDE // G_CHUNK


def _to_wide(full, blk):
    out = full[0:G_CHUNK]
    for cc in range(1, G_CPB):
        out = jnp.where(blk == cc, full[cc * G_CHUNK:(cc + 1) * G_CHUNK], out)
    return out


def _col_to_wide(col, blk):
    out = col[0:G_CHUNK]
    for cc in range(1, G_CPB):
        out = jnp.where(blk == cc, col[cc * G_CHUNK:(cc + 1) * G_CHUNK], out)
    return out


def _block_diag(wide, blk):
    zero = jnp.zeros_like(wide)
    return jnp.concatenate([jnp.where(blk == cc, wide, zero) for cc in range(G_CPB)], axis=0)


def _gdn_chunk_kernel(q_ref, k_ref, v_ref, z_ref, bg_ref, bgt_ref, ng_ref, out_ref, st_s, rows_s,
                      *, tc, heads):
    hg = pl.program_id(1)
    c = pl.program_id(2)
    nblk = tc // G_WIDE

    @pl.when(c == 0)
    def _():
        st_s[...] = jnp.zeros(st_s.shape, F32)

    ri = lax.broadcasted_iota(jnp.int32, (G_WIDE, G_WIDE), 0)
    ci = lax.broadcasted_iota(jnp.int32, (G_WIDE, G_WIDE), 1)
    same = (ri // G_CHUNK) == (ci // G_CHUNK)
    ones_incl = jnp.where(jnp.logical_and(same, ci <= ri), 1.0, 0.0).astype(BF16)
    ones_inclT = jnp.where(jnp.logical_and(same, ri <= ci), 1.0, 0.0).astype(BF16)
    ones_same = jnp.where(same, 1.0, 0.0).astype(BF16)

    wi = lax.broadcasted_iota(jnp.int32, (G_CHUNK, G_WIDE), 0)
    wl = lax.broadcasted_iota(jnp.int32, (G_CHUNK, G_WIDE), 1)
    wblk = wl // G_CHUNK
    wj = wl % G_CHUNK
    incl_w = wj <= wi
    strict_w = wj < wi
    eye_w = jnp.where(wj == wi, 1.0, 0.0)
    kblk = lax.broadcasted_iota(jnp.int32, (G_DK, G_WIDE), 1) // G_CHUNK

    bg = bg_ref[...]
    bgt = bgt_ref[...]
    gc_cols, gc_rows, gt_rows = [], [], []
    for b in range(nblk):
        sl = slice(b * G_WIDE, (b + 1) * G_WIDE)
        gcc = jnp.zeros((G_WIDE, LANES), F32)
        for piece in _split3(bg[sl]):
            gcc = gcc + _dot(ones_incl, piece)
        gc_cols.append(gcc)
        gcr = jnp.zeros((2 * G_V, G_WIDE), F32)
        gtr = jnp.zeros((2 * G_V, G_WIDE), F32)
        for piece in _split3(bgt[:, sl]):
            gcr = gcr + _dot(piece, ones_inclT)
            gtr = gtr + _dot(piece, ones_same)
        gc_rows.append(gcr)
        gt_rows.append(gtr)
    gc_all = jnp.concatenate(gc_cols, axis=0)
    rows_s[0:2 * G_V, :] = bgt
    rows_s[2 * G_V:4 * G_V, :] = jnp.concatenate(gc_rows, axis=1)
    rows_s[4 * G_V:6 * G_V, :] = jnp.concatenate(gt_rows, axis=1)

    lane = lax.broadcasted_iota(jnp.int32, (tc, LANES), 1)
    ngain = ng_ref[...]

    qk_heads = []
    for qh in range(heads // G_PAIR):
        q = q_ref[:, qh * G_DK:(qh + 1) * G_DK].astype(F32)
        k = k_ref[:, qh * G_DK:(qh + 1) * G_DK].astype(F32)
        qn = q * lax.rsqrt(jnp.sum(q * q, axis=-1, keepdims=True) + EPS) * (G_DK ** -0.5)
        kn = k * lax.rsqrt(jnp.sum(k * k, axis=-1, keepdims=True) + EPS)
        qb = qn.astype(BF16)
        kb = kn.astype(BF16)
        kn_t = kn.T
        kk_w, qk_w = [], []
        for b in range(nblk):
            sl = slice(b * G_WIDE, (b + 1) * G_WIDE)
            kk_w.append(_to_wide(_dot_nt(kb[sl], kb[sl]), wblk))
            qk_w.append(_to_wide(_dot_nt(qb[sl], kb[sl]), wblk))
        qk_heads.append((qn, kn, kn_t, kk_w, qk_w))

    per_head = {}
    chains = {}

    def st_prep(hs):
        for hh in hs:
            head = hg * heads + hh
            qn, kn, kn_t, kk_w, qk_w = qk_heads[hh // G_PAIR]
            pick = lambda a, off: jnp.sum(jnp.where(lane == head + off, a, 0.0), axis=-1,
                                          keepdims=True)
            beta_c = pick(bg, 0)
            gc_c = pick(gc_all, G_V)
            beta_r = rows_s[pl.ds(head, 1), :]
            gc_r = rows_s[pl.ds(3 * G_V + head, 1), :]
            gt_r = rows_s[pl.ds(5 * G_V + head, 1), :]
            e_gc_c = jnp.exp(gc_c)
            vk = jnp.concatenate([v_ref[:, hh * G_DV:(hh + 1) * G_DV],
                                  (kn * e_gc_c).astype(BF16)], axis=1)
            per_head[hh] = dict(beta_r=beta_r, e_gt_r=jnp.exp(gt_r),
                                e_rest_r=jnp.exp(gt_r - gc_r), qd=qn * e_gc_c, vk=vk, kn_t=kn_t)
            for b in range(nblk):
                sl = slice(b * G_WIDE, (b + 1) * G_WIDE)
                decay_w = jnp.exp(jnp.where(incl_w, _col_to_wide(gc_c[sl], wblk) - gc_r[:, sl],
                                            NEG_BIG))
                n_w = jnp.where(strict_w,
                                -(_col_to_wide(beta_c[sl], wblk) * kk_w[b] * decay_w), 0.0)
                qkm = jnp.where(incl_w, qk_w[b] * decay_w, 0.0).astype(BF16)
                chains[hh, b] = dict(hh=hh, sl=sl, n_w=n_w, qkm=qkm)

    def group_chains(hs):
        return [chains[hh, b] for hh in hs for b in range(nblk)]

    def st_first(hs):
        for ch in group_chains(hs):
            ch["t_w"] = eye_w + ch["n_w"]
            npow = ch["n_w"].astype(BF16)
            ch["nsq"] = _dot(npow, _block_diag(npow, wblk))

    def st_level(hs):
        for ch in group_chains(hs):
            npow = ch["nsq"].astype(BF16)
            both = _dot(jnp.concatenate([npow, ch["t_w"].astype(BF16)], axis=0),
                        _block_diag(npow, wblk))
            ch["nsq"] = both[:G_CHUNK]
            ch["t_w"] = ch["t_w"] + both[G_CHUNK:]

    def st_last(hs):
        for ch in group_chains(hs):
            ch["t_w"] = ch["t_w"] + _dot(ch["t_w"].astype(BF16),
                                         _block_diag(ch["nsq"].astype(BF16), wblk))

    def st_uw(hs):
        for ch in group_chains(hs):
            ph, sl = per_head[ch["hh"]], ch["sl"]
            t_beta = (ch["t_w"] * ph["beta_r"][:, sl]).astype(BF16)
            ch["uw"] = _dot(_block_diag(t_beta, wblk), ph["vk"][sl]).astype(BF16)

    def st_big(hs):
        for ch in group_chains(hs):
            ph, sl = per_head[ch["hh"]], ch["sl"]
            kd_t = (ph["kn_t"][:, sl] * ph["e_rest_r"][:, sl]).astype(BF16)
            zero = jnp.zeros_like(kd_t)
            lhs = jnp.concatenate([_block_diag(ch["qkm"], wblk)]
                                  + [jnp.where(kblk == cc, kd_t, zero) for cc in range(G_CPB)],
                                  axis=0)
            big = _dot(lhs, ch["uw"])
            ch["qp"] = (ph["qd"][sl] - big[:G_WIDE, G_DV:]).astype(BF16)
            ch["qu"] = big[:G_WIDE, :G_DV]
            ch["wk"] = [big[G_WIDE + cc * G_DK:G_WIDE + (cc + 1) * G_DK, G_DV:].astype(BF16)
                        for cc in range(G_CPB)]
            ch["uk"] = [big[G_WIDE + cc * G_DK:G_WIDE + (cc + 1) * G_DK, :G_DV]
                        for cc in range(G_CPB)]

    def st_recur(hs):
        states = {hh: st_s[hh] for hh in hs}
        outs = {hh: [] for hh in hs}
        for b in range(nblk):
            for cc in range(G_CPB):
                rows = slice(cc * G_CHUNK, (cc + 1) * G_CHUNK)
                t0 = b * G_WIDE + cc * G_CHUNK
                for hh in hs:
                    ch = chains[hh, b]
                    xs = _dot(jnp.concatenate([ch["wk"][cc], ch["qp"][rows]], axis=0),
                              states[hh].astype(BF16))
                    outs[hh].append(xs[G_DK:] + ch["qu"][rows])
                    states[hh] = (states[hh] * per_head[hh]["e_gt_r"][:, t0:t0 + 1]
                                  - xs[:G_DK] + ch["uk"][cc])
        for hh in hs:
            st_s[hh] = states[hh]
            o = jnp.concatenate(outs[hh], axis=0)
            z = z_ref[:, hh * G_DV:(hh + 1) * G_DV].astype(F32)
            y = _rms(o, ngain) * _silu(z)
            out_ref[:, hh * G_DV:(hh + 1) * G_DV] = y.astype(BF16)

    for stage in [st_prep, st_first] + [st_level] * 4 + [st_last, st_uw, st_big, st_recur]:
        stage(list(range(heads)))


def _gdn_chunk(qkv, z, bg, bgt, ngain, *, batch, seq, tc, heads):
    m = qkv.shape[0]
    nc = seq // tc
    kern = functools.partial(_gdn_chunk_kernel, tc=tc, heads=heads)
    qk_w = heads // G_PAIR * G_DK
    v_w = heads * G_DV
    rowi = lambda b, g, c: b * nc + c
    return pl.pallas_call(
        kern,
        grid=(batch, G_V // heads, nc),
        in_specs=[
            pl.BlockSpec((tc, qk_w), lambda b, g, c: (rowi(b, g, c), g)),
            pl.BlockSpec((tc, qk_w), lambda b, g, c: (rowi(b, g, c), G_KEY // qk_w + g)),
            pl.BlockSpec((tc, v_w), lambda b, g, c: (rowi(b, g, c), 2 * G_KEY // v_w + g)),
            pl.BlockSpec((tc, v_w), lambda b, g, c: (rowi(b, g, c), g)),
            pl.BlockSpec((tc, LANES), lambda b, g, c: (rowi(b, g, c), 0)),
            pl.BlockSpec((None, 2 * G_V, tc), lambda b, g, c: (b, 0, c)),
            pl.BlockSpec((1, G_DV), lambda b, g, c: (0, 0)),
        ],
        out_specs=pl.BlockSpec((tc, v_w), lambda b, g, c: (rowi(b, g, c), g)),
        out_shape=jax.ShapeDtypeStruct((m, G_VAL), BF16),
        scratch_shapes=[pltpu.VMEM((heads, G_DK, G_DV), F32),
                        pltpu.VMEM((6 * G_V, tc), F32)],
        compiler_params=pltpu.CompilerParams(
            dimension_semantics=("arbitrary", "arbitrary", "arbitrary"),
            vmem_limit_bytes=VMEM_LIMIT),
        name="gdn_chunk",
    )(qkv, qkv, qkv, z, bg, bgt, ngain)


def _rope_tables(positions):
    half = ROPE_DIM // 2
    inv_freq = ROPE_THETA ** (-jnp.arange(0, ROPE_DIM, 2, dtype=F32) / ROPE_DIM)
    ang = positions.astype(F32).reshape(-1, 1) * inv_freq
    cos, sin = jnp.cos(ang), jnp.sin(ang)
    n = ang.shape[0]
    cos_h = jnp.concatenate([cos, cos, jnp.ones((n, A_HD - 2 * half), F32)], axis=1)
    sin_h = jnp.concatenate([-sin, sin, jnp.zeros((n, A_HD - 2 * half), F32)], axis=1)
    return jnp.tile(cos_h, (1, LANES // A_HD)), jnp.tile(sin_h, (1, LANES // A_HD))


def kernel(x, positions, mixer_norm, ffn_norm, attn_w_in, attn_q_gain, attn_k_gain, attn_sinks,
           attn_w_out, gdn_w_in, gdn_conv, gdn_a_log, gdn_dt_bias, gdn_norm, gdn_w_out,
           ffn_w_up, ffn_conv, ffn_conv_b, ffn_w_down):
    batch, seq, _ = x.shape
    depth = mixer_norm.shape[0]
    m = batch * seq
    attn_tile = min(512, seq)
    row_tile = min(512, seq)
    gdn_tc = min(512, seq)

    cosf, sins = _rope_tables(positions)
    attn_w_in_b = attn_w_in.astype(BF16)
    attn_w_out_b = attn_w_out.astype(BF16)
    gdn_w_main_b = gdn_w_in.astype(BF16)
    gdn_w_out_b = gdn_w_out.astype(BF16)
    ffn_w_up_b = ffn_w_up.astype(BF16)
    ffn_w_down_b = ffn_w_down.astype(BF16)

    h = x.reshape(m, D_MODEL)
    for i in range(depth):
        j = i // 2
        gain = mixer_norm[i].reshape(1, D_MODEL)
        mixer = None
        if i % 2 == 0:
            qg = jnp.tile(attn_q_gain[j].reshape(1, A_HD), (1, LANES // A_HD)) * (A_HD ** -0.5)
            kg = jnp.tile(attn_k_gain[j].reshape(1, A_HD), (1, LANES // A_HD))
            h = _attn_layer(h, gain, attn_w_in_b, attn_w_out_b, cosf, sins, qg, kg,
                            attn_sinks[j].astype(F32), layer=j, seq=seq, tile=attn_tile)
        else:
            w_ba = jnp.pad(gdn_w_in[j][:, G_MAIN:], ((0, 0), (0, LANES - 2 * G_V))).astype(BF16)
            pad16 = lambda a: jnp.pad(a.astype(F32).reshape(1, G_V), ((0, 0), (G_V, LANES - 2 * G_V)))
            qkv, z, bg = _gdn_in_proj(h, gain, gdn_w_main_b, w_ba, gdn_conv[j],
                                      pad16(gdn_a_log[j]), pad16(gdn_dt_bias[j]),
                                      layer=j, seq=seq, tm=row_tile, tn=256)
            bgt = jnp.transpose(bg.reshape(batch, seq, LANES)[:, :, :2 * G_V], (0, 2, 1))
            o = _gdn_chunk(qkv, z, bg, bgt, gdn_norm[j].reshape(1, G_DV), batch=batch, seq=seq,
                           tc=gdn_tc, heads=8)
            mixer = (o, gdn_w_out_b, j)
        h = _ffn_layer(h, ffn_norm[i].reshape(1, D_MODEL), ffn_w_up_b, ffn_conv[i],
                       ffn_conv_b[i].reshape(1, D_FF), ffn_w_down_b,
                       layer=i, seq=seq, tm=row_tile, tf=256, mixer=mixer)
    return h.reshape(batch, seq, D_MODEL)
```

```python
import functools

import jax
import jax.numpy as jnp
from jax import lax
from jax.experimental import pallas as pl
from jax.experimental.pallas import tpu as pltpu

F32 = jnp.float32
BF16 = jnp.bfloat16
EPS = 1e-6

D_MODEL = 1024
A_HEADS = 16
A_KV = 4
A_GROUP = A_HEADS // A_KV
A_HD = 64
A_Q = A_HEADS * A_HD
A_KVD = A_KV * A_HD
A_BLK = 128
ROPE_DIM = A_HD // 4
ROPE_THETA = 500000.0
G_QK = 8
G_V = 16
G_DK = 128
G_DV = 128
G_CONV = 4
G_CHUNK = 64
G_KEY = G_QK * G_DK
G_VAL = G_V * G_DV
G_QKV = 2 * G_KEY + G_VAL
G_MAIN = G_QKV + G_VAL
D_FF = 3 * D_MODEL
F_CONV = 3

LANES = 128
SUBLANES = 8
VMEM_LIMIT = 56 * 1024 * 1024
NEG_BIG = -1e30


def _dot(a, b):
    return jnp.dot(a, b, preferred_element_type=F32)


def _dot_nt(a, b):
    return lax.dot_general(a, b, (((1,), (1,)), ((), ())), preferred_element_type=F32)


def _dot_tn(a, b):
    return lax.dot_general(a, b, (((0,), (0,)), ((), ())), preferred_element_type=F32)


def _rms(x, gain):
    ms = jnp.mean(x * x, axis=-1, keepdims=True)
    return x * lax.rsqrt(ms + EPS) * gain


def _silu(x):
    hx = 0.5 * x
    return hx * (1.0 + jnp.tanh(hx))


def _split3(x):
    a = x.astype(BF16)
    r = x - a.astype(F32)
    b = r.astype(BF16)
    c = (r - b.astype(F32)).astype(BF16)
    return a, b, c


def _norm_rope_pair(x, gain, cosf, sins, lane):
    x2 = x * x
    lo = lane < A_HD
    s_lo = jnp.sum(jnp.where(lo, x2, 0.0), axis=-1, keepdims=True)
    s_hi = jnp.sum(jnp.where(lo, 0.0, x2), axis=-1, keepdims=True)
    ms = jnp.where(lo, s_lo, s_hi) * (1.0 / A_HD)
    y = x * lax.rsqrt(ms + EPS) * gain
    rot = jnp.where((lane % A_HD) < ROPE_DIM // 2,
                    pltpu.roll(y, LANES - ROPE_DIM // 2, 1), pltpu.roll(y, ROPE_DIM // 2, 1))
    return y * cosf + rot * sins


def _attn_kernel(sink_ref, h_ref, gain_ref, win_ref, wout_ref, cos_ref, sin_ref, qg_ref, kg_ref,
                 out_ref, q_s, klo_s, khi_s, vlo_s, vhi_s, o_s, *, tile, tiles_per_seq):
    step = pl.program_id(0)
    first = (step % tiles_per_seq) == 0
    nblk = tile // A_BLK
    kv_bufs = (klo_s, khi_s, vlo_s, vhi_s)

    @pl.when(first)
    def _():
        for buf in kv_bufs:
            buf[0:A_BLK, :] = jnp.zeros((A_BLK, A_KV * LANES), BF16)

    @pl.when(jnp.logical_not(first))
    def _():
        for buf in kv_bufs:
            buf[0:A_BLK, :] = buf[tile:tile + A_BLK, :]

    x = h_ref[...]
    xn = _rms(x, gain_ref[...]).astype(BF16)
    qkv = _dot(xn, win_ref[...])

    cosf = cos_ref[...]
    sins = sin_ref[...]
    lane = lax.broadcasted_iota(jnp.int32, (tile, LANES), 1)
    low = lane < A_HD
    qg = qg_ref[...]
    kg = kg_ref[...]
    for p in range(A_Q // LANES):
        c0 = p * LANES
        qp = _norm_rope_pair(qkv[:, c0:c0 + LANES], qg, cosf, sins, lane)
        q_s[:, c0:c0 + LANES] = qp.astype(BF16)

    def place(pair, lo_s, hi_s, p):
        swapped = pltpu.roll(pair, A_HD, 1)
        rows = slice(A_BLK, A_BLK + tile)
        g0, g1 = 2 * p, 2 * p + 1
        lo_s[rows, g0 * LANES:(g0 + 1) * LANES] = jnp.where(low, pair, 0.0).astype(BF16)
        hi_s[rows, g0 * LANES:(g0 + 1) * LANES] = jnp.where(low, 0.0, swapped).astype(BF16)
        lo_s[rows, g1 * LANES:(g1 + 1) * LANES] = jnp.where(low, swapped, 0.0).astype(BF16)
        hi_s[rows, g1 * LANES:(g1 + 1) * LANES] = jnp.where(low, 0.0, pair).astype(BF16)

    for p in range(A_KVD // LANES):
        c0 = A_Q + p * LANES
        place(_norm_rope_pair(qkv[:, c0:c0 + LANES], kg, cosf, sins, lane), klo_s, khi_s, p)
        place(qkv[:, c0 + A_KVD:c0 + A_KVD + LANES], vlo_s, vhi_s, p)

    rows2 = 2 * A_BLK
    qi = lax.broadcasted_iota(jnp.int32, (rows2, A_BLK), 0) % A_BLK
    kj = lax.broadcasted_iota(jnp.int32, (rows2, A_BLK), 1)
    causal = kj <= qi
    ones = jnp.ones((2 * A_BLK, LANES), BF16)

    bufs = ((klo_s, vlo_s), (khi_s, vhi_s))
    bodies = [(g, par) for g in range(A_KV) for par in range(2)]
    for blk in range(nblk):
        r0 = blk * A_BLK
        has_prev = jnp.logical_not(first) if blk == 0 else None
        scores, pps, scale_terms, rs = {}, {}, {}, {}
        for g, par in bodies:
            c0 = g * A_GROUP * A_HD
            qst = jnp.concatenate([q_s[r0:r0 + A_BLK, c0:c0 + LANES],
                                   q_s[r0:r0 + A_BLK, c0 + LANES:c0 + 2 * LANES]], axis=0)
            kb = bufs[par][0][r0:r0 + 2 * A_BLK, g * LANES:(g + 1) * LANES]
            scores[g, par] = _dot_nt(qst, kb)
        for g, par in bodies:
            s = scores[g, par]
            s_prev = s[:, :A_BLK]
            if has_prev is not None:
                s_prev = jnp.where(has_prev, s_prev, NEG_BIG)
            sc = jnp.where(causal, s[:, A_BLK:], s_prev)
            sink = jnp.concatenate(
                [jnp.full((A_BLK, 1), sink_ref[g * A_GROUP + 2 * pr + par], F32)
                 for pr in range(2)], axis=0)
            m = jnp.maximum(jnp.max(sc, axis=-1, keepdims=True), sink)
            mb = jnp.broadcast_to(m, sc.shape)
            p = jnp.exp(sc - mb)
            pps[g, par] = jnp.concatenate([jnp.where(causal, 0.0, p), jnp.where(causal, p, 0.0)],
                                          axis=1).astype(BF16)
            scale_terms[g, par] = jnp.exp(jnp.broadcast_to(sink, sc.shape) - mb)
        for g, par in bodies:
            vb = bufs[par][1][r0:r0 + 2 * A_BLK, g * LANES:(g + 1) * LANES]
            rs[g, par] = _dot(pps[g, par], jnp.concatenate([vb, ones], axis=1))
        for g in range(A_KV):
            c0 = g * A_GROUP * A_HD
            acc = None
            for par in range(2):
                r = rs[g, par]
                o = r[:, :LANES] * (1.0 / (r[:, LANES:] + scale_terms[g, par]))
                acc = o if acc is None else acc + o
            o_s[r0:r0 + A_BLK, c0:c0 + LANES] = acc[:A_BLK].astype(BF16)
            o_s[r0:r0 + A_BLK, c0 + LANES:c0 + 2 * LANES] = acc[A_BLK:].astype(BF16)

    out_ref[...] = x + _dot(o_s[...], wout_ref[...])


def _attn_layer(h, gain, w_in, w_out, cosf, sins, qg, kg, sinks, *, layer, seq, tile):
    m = h.shape[0]
    tiles_per_seq = seq // tile
    kern = functools.partial(_attn_kernel, tile=tile, tiles_per_seq=tiles_per_seq)
    row = lambda i: (i, 0)
    fixed = lambda i: (0, 0)
    return pl.pallas_call(
        kern,
        grid=(m // tile,),
        in_specs=[
            pl.BlockSpec(memory_space=pltpu.SMEM),
            pl.BlockSpec((tile, D_MODEL), row),
            pl.BlockSpec((1, D_MODEL), fixed),
            _resident((D_MODEL, A_Q + 2 * A_KVD), layer),
            _resident((A_Q, D_MODEL), layer),
            pl.BlockSpec((tile, LANES), row),
            pl.BlockSpec((tile, LANES), row),
            pl.BlockSpec((1, LANES), fixed),
            pl.BlockSpec((1, LANES), fixed),
        ],
        out_specs=pl.BlockSpec((tile, D_MODEL), row),
        out_shape=jax.ShapeDtypeStruct((m, D_MODEL), F32),
        scratch_shapes=[
            pltpu.VMEM((tile, A_Q), BF16),
            pltpu.VMEM((tile + A_BLK, A_KV * LANES), BF16),
            pltpu.VMEM((tile + A_BLK, A_KV * LANES), BF16),
            pltpu.VMEM((tile + A_BLK, A_KV * LANES), BF16),
            pltpu.VMEM((tile + A_BLK, A_KV * LANES), BF16),
            pltpu.VMEM((tile, A_Q), BF16),
        ],
        compiler_params=pltpu.CompilerParams(dimension_semantics=("arbitrary",),
                                             vmem_limit_bytes=VMEM_LIMIT),
        name="attn_layer",
    )(sinks, h, gain, w_in, w_out, cosf, sins, qg, kg)


CONV_ROWS = 256
LOOKAHEAD = 2


def _conv_stage(vals, ext, first):
    tm = vals.shape[0]
    ext[0:SUBLANES, :] = jnp.where(first, 0.0, ext[tm:tm + SUBLANES, :])
    ext[SUBLANES:SUBLANES + tm, :] = vals


def _conv_block(ext, r, cw, taps):
    xe = ext[r:r + SUBLANES + CONV_ROWS, :]
    if taps == 4:
        xs = pltpu.roll(xe, 1, 0)
        older = xs * cw[0:1, :] + xe * cw[1:2, :]
        newer = xs * cw[2:3, :] + xe * cw[3:4, :]
        return (pltpu.roll(older, 2, 0) + newer)[SUBLANES:]
    z = xe * cw[0:1, :]
    for k in range(1, taps):
        z = pltpu.roll(z, 1, 0) + xe * cw[k:k + 1, :]
    return z[SUBLANES:]


def _resident(shape, layer=None):
    if layer is None:
        return pl.BlockSpec(shape, lambda *_: (0,) * len(shape), pipeline_mode=pl.Buffered(1))
    return pl.BlockSpec((None,) + tuple(shape), lambda *_: (layer,) + (0,) * len(shape),
                        pipeline_mode=pl.Buffered(1))


def _ffn_kernel(*refs, tiles_per_seq, tf, down_every, mixer_out):
    if mixer_out:
        mix_ref, wmix_ref, *refs = refs
    h_ref, gain_ref, wup_ref, cw_ref, cb_ref, wd_ref, out_ref, ext_s, up_s, act_s = refs
    i = pl.program_id(0)
    first = (i % tiles_per_seq) == 0
    tm = h_ref.shape[0]
    nj = D_FF // tf

    @pl.when(i == 0)
    def _():
        ext_s[:, tm:tm + SUBLANES, :] = jnp.zeros((nj, SUBLANES, tf), F32)

    x = h_ref[...]
    if mixer_out:
        x = x + _dot(mix_ref[...], wmix_ref[...])
    xn = _rms(x, gain_ref[...]).astype(BF16)

    def produce(j):
        _conv_stage(_dot(xn, wup_ref[:, j * tf:(j + 1) * tf]), ext_s.at[j], first)
        up_s[j] = _dot(xn, wup_ref[:, D_FF + j * tf:D_FF + (j + 1) * tf])

    acc = x
    for j in range(min(LOOKAHEAD, nj)):
        produce(j)
    for j in range(nj):
        if j + LOOKAHEAD < nj:
            produce(j + LOOKAHEAD)
        cols = slice(j * tf, (j + 1) * tf)
        cw = cw_ref[:, cols]
        cb = cb_ref[:, cols]
        for r in range(0, tm, CONV_ROWS):
            conv = _conv_block(ext_s.at[j], r, cw, F_CONV) + cb
            act_s[r:r + CONV_ROWS, cols] = (_silu(conv) * up_s[j, r:r + CONV_ROWS, :]).astype(BF16)
        if (j + 1) % down_every == 0:
            kc = slice((j + 1 - down_every) * tf, (j + 1) * tf)
            acc = acc + _dot(act_s[:, kc], wd_ref[kc, :])
    out_ref[...] = acc


def _ffn_layer(h, gain, w_up, conv_w, conv_b, w_down, *, layer, seq, tm, tf, mixer=None):
    m = h.shape[0]
    nj = D_FF // tf
    kern = functools.partial(_ffn_kernel, tiles_per_seq=seq // tm, tf=tf,
                             down_every=D_MODEL // tf, mixer_out=mixer is not None)
    mixer_specs, mixer_args = [], []
    if mixer is not None:
        mix, w_mix, mix_layer = mixer
        kdim = mix.shape[1]
        mixer_specs = [pl.BlockSpec((tm, kdim), lambda i: (i, 0)),
                       _resident((kdim, D_MODEL), mix_layer)]
        mixer_args = [mix, w_mix]
    return pl.pallas_call(
        kern,
        grid=(m // tm,),
        in_specs=mixer_specs + [
            pl.BlockSpec((tm, D_MODEL), lambda i: (i, 0)),
            _resident((1, D_MODEL)),
            _resident((D_MODEL, 2 * D_FF), layer),
            _resident((F_CONV, D_FF)),
            _resident((1, D_FF)),
            _resident((D_FF, D_MODEL), layer),
        ],
        out_specs=pl.BlockSpec((tm, D_MODEL), lambda i: (i, 0)),
        out_shape=jax.ShapeDtypeStruct((m, D_MODEL), F32),
        scratch_shapes=[pltpu.VMEM((nj, tm + SUBLANES, tf), F32),
                        pltpu.VMEM((nj, tm, tf), F32),
                        pltpu.VMEM((tm, D_FF), BF16)],
        compiler_params=pltpu.CompilerParams(dimension_semantics=("arbitrary",),
                                             vmem_limit_bytes=VMEM_LIMIT),
        name="ffn_layer",
    )(*mixer_args, h, gain, w_up, conv_w, conv_b, w_down)


def _gdn_in_kernel(h_ref, gain_ref, w_ref, wba_ref, cw_ref, alog_ref, dtb_ref,
                   qkv_ref, z_ref, bg_ref, ext_s, *, tiles_per_seq, tn):
    i = pl.program_id(0)
    first = (i % tiles_per_seq) == 0
    tm = h_ref.shape[0]
    n_conv = G_QKV // tn
    n_all = G_MAIN // tn

    @pl.when(i == 0)
    def _():
        ext_s[:, tm:tm + SUBLANES, :] = jnp.zeros((n_conv, SUBLANES, tn), F32)

    xn = _rms(h_ref[...], gain_ref[...]).astype(BF16)
    ba = _dot(xn, wba_ref[...])
    lane = lax.broadcasted_iota(jnp.int32, ba.shape, 1)
    beta = 1.0 / (1.0 + jnp.exp(-ba))
    sp_in = ba + dtb_ref[...]
    softplus = jnp.maximum(sp_in, 0.0) + jnp.log1p(jnp.exp(-jnp.abs(sp_in)))
    g = -jnp.exp(alog_ref[...]) * softplus
    bg_ref[...] = jnp.where(lane < G_V, beta, jnp.where(lane < 2 * G_V, g, 0.0))

    def produce(j):
        cols = slice(j * tn, (j + 1) * tn)
        proj = _dot(xn, w_ref[:, cols])
        if j < n_conv:
            _conv_stage(proj, ext_s.at[j], first)
        else:
            z_ref[:, (j - n_conv) * tn:(j - n_conv + 1) * tn] = proj.astype(BF16)

    n_z = n_all - n_conv
    per_z = n_conv // n_z
    order = []
    for zc in range(n_z):
        order += list(range(zc * per_z, (zc + 1) * per_z)) + [n_conv + zc]
    for j in order[:LOOKAHEAD]:
        produce(j)
    for pos, j in enumerate(order):
        if pos + LOOKAHEAD < n_all:
            produce(order[pos + LOOKAHEAD])
        if j < n_conv:
            cols = slice(j * tn, (j + 1) * tn)
            cw = cw_ref[:, cols]
            for r in range(0, tm, CONV_ROWS):
                qkv_ref[r:r + CONV_ROWS, cols] = _silu(
                    _conv_block(ext_s.at[j], r, cw, G_CONV)).astype(BF16)


def _gdn_in_proj(h, gain, w_main, w_ba, conv_w, alog, dtb, *, layer, seq, tm, tn):
    m = h.shape[0]
    kern = functools.partial(_gdn_in_kernel, tiles_per_seq=seq // tm, tn=tn)
    cur_tile = lambda i: (i, 0)
    prev_tile = cur_tile
    return pl.pallas_call(
        kern,
        grid=(m // tm,),
        in_specs=[
            pl.BlockSpec((tm, D_MODEL), cur_tile),
            _resident((1, D_MODEL)),
            _resident((D_MODEL, G_MAIN), layer),
            _resident((D_MODEL, LANES)),
            _resident((G_CONV, G_QKV)),
            _resident((1, LANES)),
            _resident((1, LANES)),
        ],
        out_specs=[
            pl.BlockSpec((tm, G_QKV), prev_tile),
            pl.BlockSpec((tm, G_VAL), cur_tile),
            pl.BlockSpec((tm, LANES), cur_tile),
        ],
        out_shape=[
            jax.ShapeDtypeStruct((m, G_QKV), BF16),
            jax.ShapeDtypeStruct((m, G_VAL), BF16),
            jax.ShapeDtypeStruct((m, LANES), F32),
        ],
        scratch_shapes=[pltpu.VMEM((G_QKV // tn, tm + SUBLANES, tn), F32)],
        compiler_params=pltpu.CompilerParams(dimension_semantics=("arbitrary",),
                                             vmem_limit_bytes=VMEM_LIMIT),
        name="gdn_in_proj",
    )(h, gain, w_main, w_ba, conv_w, alog, dtb)


G_PAIR = G_V // G_QK


G_WIDE = 256
G_CPB = G_WIDE // G_CHUNK


def _to_wide(full, blk):
    out = full[0:G_CHUNK]
    for cc in range(1, G_CPB):
        out = jnp.where(blk == cc, full[cc * G_CHUNK:(cc + 1) * G_CHUNK], out)
    return out


def _col_to_wide(col, blk):
    out = col[0:G_CHUNK]
    for cc in range(1, G_CPB):
        out = jnp.where(blk == cc, col[cc * G_CHUNK:(cc + 1) * G_CHUNK], out)
    return out


def _block_diag(wide, blk):
    zero = jnp.zeros_like(wide)
    return jnp.concatenate([jnp.where(blk == cc, wide, zero) for cc in range(G_CPB)], axis=0)


def _gdn_chunk_kernel(q_ref, k_ref, v_ref, z_ref, bg_ref, bgt_ref, ng_ref, out_ref, st_s, rows_s,
                      *, tc, heads):
    hg = pl.program_id(1)
    c = pl.program_id(2)
    nblk = tc // G_WIDE

    @pl.when(c == 0)
    def _():
        st_s[...] = jnp.zeros(st_s.shape, F32)

    ri = lax.broadcasted_iota(jnp.int32, (G_WIDE, G_WIDE), 0)
    ci = lax.broadcasted_iota(jnp.int32, (G_WIDE, G_WIDE), 1)
    same = (ri // G_CHUNK) == (ci // G_CHUNK)
    ones_incl = jnp.where(jnp.logical_and(same, ci <= ri), 1.0, 0.0).astype(BF16)
    ones_inclT = jnp.where(jnp.logical_and(same, ri <= ci), 1.0, 0.0).astype(BF16)
    ones_same = jnp.where(same, 1.0, 0.0).astype(BF16)

    wi = lax.broadcasted_iota(jnp.int32, (G_CHUNK, G_WIDE), 0)
    wl = lax.broadcasted_iota(jnp.int32, (G_CHUNK, G_WIDE), 1)
    wblk = wl // G_CHUNK
    wj = wl % G_CHUNK
    incl_w = wj <= wi
    strict_w = wj < wi
    eye_w = jnp.where(wj == wi, 1.0, 0.0)
    kblk = lax.broadcasted_iota(jnp.int32, (G_DK, G_WIDE), 1) // G_CHUNK

    bg = bg_ref[...]
    bgt = bgt_ref[...]
    gc_cols, gc_rows, gt_rows = [], [], []
    for b in range(nblk):
        sl = slice(b * G_WIDE, (b + 1) * G_WIDE)
        gcc = jnp.zeros((G_WIDE, LANES), F32)
        for piece in _split3(bg[sl]):
            gcc = gcc + _dot(ones_incl, piece)
        gc_cols.append(gcc)
        gcr = jnp.zeros((2 * G_V, G_WIDE), F32)
        gtr = jnp.zeros((2 * G_V, G_WIDE), F32)
        for piece in _split3(bgt[:, sl]):
            gcr = gcr + _dot(piece, ones_inclT)
            gtr = gtr + _dot(piece, ones_same)
        gc_rows.append(gcr)
        gt_rows.append(gtr)
    gc_all = jnp.concatenate(gc_cols, axis=0)
    rows_s[0:2 * G_V, :] = bgt
    rows_s[2 * G_V:4 * G_V, :] = jnp.concatenate(gc_rows, axis=1)
    rows_s[4 * G_V:6 * G_V, :] = jnp.concatenate(gt_rows, axis=1)

    lane = lax.broadcasted_iota(jnp.int32, (tc, LANES), 1)
    ngain = ng_ref[...]

    qk_heads = []
    for qh in range(heads // G_PAIR):
        q = q_ref[:, qh * G_DK:(qh + 1) * G_DK].astype(F32)
        k = k_ref[:, qh * G_DK:(qh + 1) * G_DK].astype(F32)
        qn = q * lax.rsqrt(jnp.sum(q * q, axis=-1, keepdims=True) + EPS) * (G_DK ** -0.5)
        kn = k * lax.rsqrt(jnp.sum(k * k, axis=-1, keepdims=True) + EPS)
        qb = qn.astype(BF16)
        kb = kn.astype(BF16)
        kn_t = kn.T
        kk_w, qk_w = [], []
        for b in range(nblk):
            sl = slice(b * G_WIDE, (b + 1) * G_WIDE)
            kk_w.append(_to_wide(_dot_nt(kb[sl], kb[sl]), wblk))
            qk_w.append(_to_wide(_dot_nt(qb[sl], kb[sl]), wblk))
        qk_heads.append((qn, kn, kn_t, kk_w, qk_w))

    per_head = {}
    chains = {}

    def st_prep(hs):
        for hh in hs:
            head = hg * heads + hh
            qn, kn, kn_t, kk_w, qk_w = qk_heads[hh // G_PAIR]
            pick = lambda a, off: jnp.sum(jnp.where(lane == head + off, a, 0.0), axis=-1,
                                          keepdims=True)
            beta_c = pick(bg, 0)
            gc_c = pick(gc_all, G_V)
            beta_r = rows_s[pl.ds(head, 1), :]
            gc_r = rows_s[pl.ds(3 * G_V + head, 1), :]
            gt_r = rows_s[pl.ds(5 * G_V + head, 1), :]
            e_gc_c = jnp.exp(gc_c)
            vk = jnp.concatenate([v_ref[:, hh * G_DV:(hh + 1) * G_DV],
                                  (kn * e_gc_c).astype(BF16)], axis=1)
            per_head[hh] = dict(beta_r=beta_r, e_gt_r=jnp.exp(gt_r),
                                e_rest_r=jnp.exp(gt_r - gc_r), qd=qn * e_gc_c, vk=vk, kn_t=kn_t)
            for b in range(nblk):
                sl = slice(b * G_WIDE, (b + 1) * G_WIDE)
                decay_w = jnp.exp(jnp.where(incl_w, _col_to_wide(gc_c[sl], wblk) - gc_r[:, sl],
                                            NEG_BIG))
                n_w = jnp.where(strict_w,
                                -(_col_to_wide(beta_c[sl], wblk) * kk_w[b] * decay_w), 0.0)
                qkm = jnp.where(incl_w, qk_w[b] * decay_w, 0.0).astype(BF16)
                chains[hh, b] = dict(hh=hh, sl=sl, n_w=n_w, qkm=qkm)

    def group_chains(hs):
        return [chains[hh, b] for hh in hs for b in range(nblk)]

    def st_first(hs):
        for ch in group_chains(hs):
            ch["t_w"] = eye_w + ch["n_w"]
            npow = ch["n_w"].astype(BF16)
            ch["nsq"] = _dot(npow, _block_diag(npow, wblk))

    def st_level(hs):
        for ch in group_chains(hs):
            npow = ch["nsq"].astype(BF16)
            both = _dot(jnp.concatenate([npow, ch["t_w"].astype(BF16)], axis=0),
                        _block_diag(npow, wblk))
            ch["nsq"] = both[:G_CHUNK]
            ch["t_w"] = ch["t_w"] + both[G_CHUNK:]

    def st_last(hs):
        for ch in group_chains(hs):
            ch["t_w"] = ch["t_w"] + _dot(ch["t_w"].astype(BF16),
                                         _block_diag(ch["nsq"].astype(BF16), wblk))

    def st_uw(hs):
        for ch in group_chains(hs):
            ph, sl = per_head[ch["hh"]], ch["sl"]
            t_beta = (ch["t_w"] * ph["beta_r"][:, sl]).astype(BF16)
            ch["uw"] = _dot(_block_diag(t_beta, wblk), ph["vk"][sl]).astype(BF16)

    def st_big(hs):
        for ch in group_chains(hs):
            ph, sl = per_head[ch["hh"]], ch["sl"]
            kd_t = (ph["kn_t"][:, sl] * ph["e_rest_r"][:, sl]).astype(BF16)
            zero = jnp.zeros_like(kd_t)
            lhs = jnp.concatenate([_block_diag(ch["qkm"], wblk)]
                                  + [jnp.where(kblk == cc, kd_t, zero) for cc in range(G_CPB)],
                                  axis=0)
            big = _dot(lhs, ch["uw"])
            ch["qp"] = (ph["qd"][sl] - big[:G_WIDE, G_DV:]).astype(BF16)
            ch["qu"] = big[:G_WIDE, :G_DV]
            ch["wk"] = [big[G_WIDE + cc * G_DK:G_WIDE + (cc + 1) * G_DK, G_DV:].astype(BF16)
                        for cc in range(G_CPB)]
            ch["uk"] = [big[G_WIDE + cc * G_DK:G_WIDE + (cc + 1) * G_DK, :G_DV]
                        for cc in range(G_CPB)]

    def st_recur(hs):
        states = {hh: st_s[hh] for hh in hs}
        outs = {hh: [] for hh in hs}
        for b in range(nblk):
            for cc in range(G_CPB):
                rows = slice(cc * G_CHUNK, (cc + 1) * G_CHUNK)
                t0 = b * G_WIDE + cc * G_CHUNK
                for hh in hs:
                    ch = chains[hh, b]
                    xs = _dot(jnp.concatenate([ch["wk"][cc], ch["qp"][rows]], axis=0),
                              states[hh].astype(BF16))
                    outs[hh].append(xs[G_DK:] + ch["qu"][rows])
                    states[hh] = (states[hh] * per_head[hh]["e_gt_r"][:, t0:t0 + 1]
                                  - xs[:G_DK] + ch["uk"][cc])
        for hh in hs:
            st_s[hh] = states[hh]
            o = jnp.concatenate(outs[hh], axis=0)
            z = z_ref[:, hh * G_DV:(hh + 1) * G_DV].astype(F32)
            y = _rms(o, ngain) * _silu(z)
            out_ref[:, hh * G_DV:(hh + 1) * G_DV] = y.astype(BF16)

    for stage in [st_prep, st_first] + [st_level] * 4 + [st_last, st_uw, st_big, st_recur]:
        stage(list(range(heads)))


def _gdn_chunk(qkv, z, bg, bgt, ngain, *, batch, seq, tc, heads):
    m = qkv.shape[0]
    nc = seq // tc
    kern = functools.partial(_gdn_chunk_kernel, tc=tc, heads=heads)
    qk_w = heads // G_PAIR * G_DK
    v_w = heads * G_DV
    rowi = lambda b, g, c: b * nc + c
    return pl.pallas_call(
        kern,
        grid=(batch, G_V // heads, nc),
        in_specs=[
            pl.BlockSpec((tc, qk_w), lambda b, g, c: (rowi(b, g, c), g)),
            pl.BlockSpec((tc, qk_w), lambda b, g, c: (rowi(b, g, c), G_KEY // qk_w + g)),
            pl.BlockSpec((tc, v_w), lambda b, g, c: (rowi(b, g, c), 2 * G_KEY // v_w + g)),
            pl.BlockSpec((tc, v_w), lambda b, g, c: (rowi(b, g, c), g)),
            pl.BlockSpec((tc, LANES), lambda b, g, c: (rowi(b, g, c), 0)),
            pl.BlockSpec((None, 2 * G_V, tc), lambda b, g, c: (b, 0, c)),
            pl.BlockSpec((1, G_DV), lambda b, g, c: (0, 0)),
        ],
        out_specs=pl.BlockSpec((tc, v_w), lambda b, g, c: (rowi(b, g, c), g)),
        out_shape=jax.ShapeDtypeStruct((m, G_VAL), BF16),
        scratch_shapes=[pltpu.VMEM((heads, G_DK, G_DV), F32),
                        pltpu.VMEM((6 * G_V, tc), F32)],
        compiler_params=pltpu.CompilerParams(
            dimension_semantics=("arbitrary", "arbitrary", "arbitrary"),
            vmem_limit_bytes=VMEM_LIMIT),
        name="gdn_chunk",
    )(qkv, qkv, qkv, z, bg, bgt, ngain)


def _rope_tables(positions):
    half = ROPE_DIM // 2
    inv_freq = ROPE_THETA ** (-jnp.arange(0, ROPE_DIM, 2, dtype=F32) / ROPE_DIM)
    ang = positions.astype(F32).reshape(-1, 1) * inv_freq
    cos, sin = jnp.cos(ang), jnp.sin(ang)
    n = ang.shape[0]
    cos_h = jnp.concatenate([cos, cos, jnp.ones((n, A_HD - 2 * half), F32)], axis=1)
    sin_h = jnp.concatenate([-sin, sin, jnp.zeros((n, A_HD - 2 * half), F32)], axis=1)
    return jnp.tile(cos_h, (1, LANES // A_HD)), jnp.tile(sin_h, (1, LANES // A_HD))


def kernel(x, positions, mixer_norm, ffn_norm, attn_w_in, attn_q_gain, attn_k_gain, attn_sinks,
           attn_w_out, gdn_w_in, gdn_conv, gdn_a_log, gdn_dt_bias, gdn_norm, gdn_w_out,
           ffn_w_up, ffn_conv, ffn_conv_b, ffn_w_down):
    batch, seq, _ = x.shape
    depth = mixer_norm.shape[0]
    m = batch * seq
    attn_tile = min(1024, seq)
    row_tile = min(512, seq)
    gdn_tc = min(512, seq)

    cosf, sins = _rope_tables(positions)
    attn_w_in_b = attn_w_in.astype(BF16)
    attn_w_out_b = attn_w_out.astype(BF16)
    gdn_w_main_b = gdn_w_in.astype(BF16)
    gdn_w_out_b = gdn_w_out.astype(BF16)
    ffn_w_up_b = ffn_w_up.astype(BF16)
    ffn_w_down_b = ffn_w_down.astype(BF16)

    h = x.reshape(m, D_MODEL)
    for i in range(depth):
        j = i // 2
        gain = mixer_norm[i].reshape(1, D_MODEL)
        mixer = None
        if i % 2 == 0:
            qg = jnp.tile(attn_q_gain[j].reshape(1, A_HD), (1, LANES // A_HD)) * (A_HD ** -0.5)
            kg = jnp.tile(attn_k_gain[j].reshape(1, A_HD), (1, LANES // A_HD))
            h = _attn_layer(h, gain, attn_w_in_b, attn_w_out_b, cosf, sins, qg, kg,
                            attn_sinks[j].astype(F32), layer=j, seq=seq, tile=attn_tile)
        else:
            w_ba = jnp.pad(gdn_w_in[j][:, G_MAIN:], ((0, 0), (0, LANES - 2 * G_V))).astype(BF16)
            pad16 = lambda a: jnp.pad(a.astype(F32).reshape(1, G_V), ((0, 0), (G_V, LANES - 2 * G_V)))
            qkv, z, bg = _gdn_in_proj(h, gain, gdn_w_main_b, w_ba, gdn_conv[j],
                                      pad16(gdn_a_log[j]), pad16(gdn_dt_bias[j]),
                                      layer=j, seq=seq, tm=row_tile, tn=256)
            bgt = jnp.transpose(bg.reshape(batch, seq, LANES)[:, :, :2 * G_V], (0, 2, 1))
            o = _gdn_chunk(qkv, z, bg, bgt, gdn_norm[j].reshape(1, G_DV), batch=batch, seq=seq,
                           tc=gdn_tc, heads=8)
            mixer = (o, gdn_w_out_b, j)
        h = _ffn_layer(h, ffn_norm[i].reshape(1, D_MODEL), ffn_w_up_b, ffn_conv[i],
                       ffn_conv_b[i].reshape(1, D_FF), ffn_w_down_b,
                       layer=i, seq=seq, tm=row_tile, tf=256, mixer=mixer)
    return h.reshape(batch, seq, D_MODEL)
```

```python
import functools

import jax
import jax.numpy as jnp
from jax import lax
from jax.experimental import pallas as pl
from jax.experimental.pallas import tpu as pltpu

F32 = jnp.float32
BF16 = jnp.bfloat16
EPS = 1e-6

D_MODEL = 1024
A_HEADS = 16
A_KV = 4
A_GROUP = A_HEADS // A_KV
A_HD = 64
A_Q = A_HEADS * A_HD
A_KVD = A_KV * A_HD
A_BLK = 128
ROPE_DIM = A_HD // 4
ROPE_THETA = 500000.0
G_QK = 8
G_V = 16
G_DK = 128
G_DV = 128
G_CONV = 4
G_CHUNK = 64
G_KEY = G_QK * G_DK
G_VAL = G_V * G_DV
G_QKV = 2 * G_KEY + G_VAL
G_MAIN = G_QKV + G_VAL
D_FF = 3 * D_MODEL
F_CONV = 3

LANES = 128
SUBLANES = 8
VMEM_LIMIT = 56 * 1024 * 1024
NEG_BIG = -1e30


def _dot(a, b):
    return jnp.dot(a, b, preferred_element_type=F32)


def _dot_nt(a, b):
    return lax.dot_general(a, b, (((1,), (1,)), ((), ())), preferred_element_type=F32)


def _dot_tn(a, b):
    return lax.dot_general(a, b, (((0,), (0,)), ((), ())), preferred_element_type=F32)


def _rms(x, gain):
    ms = jnp.mean(x * x, axis=-1, keepdims=True)
    return x * lax.rsqrt(ms + EPS) * gain


def _silu(x):
    hx = 0.5 * x
    return hx * (1.0 + jnp.tanh(hx))


def _split3(x):
    a = x.astype(BF16)
    r = x - a.astype(F32)
    b = r.astype(BF16)
    c = (r - b.astype(F32)).astype(BF16)
    return a, b, c


def _norm_rope_pair(x, gain, cosf, sins, lane):
    x2 = x * x
    lo = lane < A_HD
    s_lo = jnp.sum(jnp.where(lo, x2, 0.0), axis=-1, keepdims=True)
    s_hi = jnp.sum(jnp.where(lo, 0.0, x2), axis=-1, keepdims=True)
    ms = jnp.where(lo, s_lo, s_hi) * (1.0 / A_HD)
    y = x * lax.rsqrt(ms + EPS) * gain
    rot = jnp.where((lane % A_HD) < ROPE_DIM // 2,
                    pltpu.roll(y, LANES - ROPE_DIM // 2, 1), pltpu.roll(y, ROPE_DIM // 2, 1))
    return y * cosf + rot * sins


def _attn_kernel(sink_ref, h_ref, gain_ref, win_ref, wout_ref, cos_ref, sin_ref, qg_ref, kg_ref,
                 out_ref, q_s, klo_s, khi_s, vlo_s, vhi_s, o_s, *, tile, tiles_per_seq):
    step = pl.program_id(0)
    first = (step % tiles_per_seq) == 0
    nblk = tile // A_BLK
    kv_bufs = (klo_s, khi_s, vlo_s, vhi_s)

    @pl.when(first)
    def _():
        for buf in kv_bufs:
            buf[0:A_BLK, :] = jnp.zeros((A_BLK, A_KV * LANES), BF16)

    @pl.when(jnp.logical_not(first))
    def _():
        for buf in kv_bufs:
            buf[0:A_BLK, :] = buf[tile:tile + A_BLK, :]

    x = h_ref[...]
    xn = _rms(x, gain_ref[...]).astype(BF16)
    qkv = _dot(xn, win_ref[...])

    cosf = cos_ref[...]
    sins = sin_ref[...]
    lane = lax.broadcasted_iota(jnp.int32, (tile, LANES), 1)
    low = lane < A_HD
    qg = qg_ref[...]
    kg = kg_ref[...]
    for p in range(A_Q // LANES):
        c0 = p * LANES
        qp = _norm_rope_pair(qkv[:, c0:c0 + LANES], qg, cosf, sins, lane)
        q_s[:, c0:c0 + LANES] = qp.astype(BF16)

    def place(pair, lo_s, hi_s, p):
        swapped = pltpu.roll(pair, A_HD, 1)
        rows = slice(A_BLK, A_BLK + tile)
        g0, g1 = 2 * p, 2 * p + 1
        lo_s[rows, g0 * LANES:(g0 + 1) * LANES] = jnp.where(low, pair, 0.0).astype(BF16)
        hi_s[rows, g0 * LANES:(g0 + 1) * LANES] = jnp.where(low, 0.0, swapped).astype(BF16)
        lo_s[rows, g1 * LANES:(g1 + 1) * LANES] = jnp.where(low, swapped, 0.0).astype(BF16)
        hi_s[rows, g1 * LANES:(g1 + 1) * LANES] = jnp.where(low, 0.0, pair).astype(BF16)

    for p in range(A_KVD // LANES):
        c0 = A_Q + p * LANES
        place(_norm_rope_pair(qkv[:, c0:c0 + LANES], kg, cosf, sins, lane), klo_s, khi_s, p)
        place(qkv[:, c0 + A_KVD:c0 + A_KVD + LANES], vlo_s, vhi_s, p)

    rows2 = 2 * A_BLK
    qi = lax.broadcasted_iota(jnp.int32, (rows2, A_BLK), 0) % A_BLK
    kj = lax.broadcasted_iota(jnp.int32, (rows2, A_BLK), 1)
    causal = kj <= qi
    ones = jnp.ones((2 * A_BLK, LANES), BF16)

    bufs = ((klo_s, vlo_s), (khi_s, vhi_s))
    bodies = [(g, par) for g in range(A_KV) for par in range(2)]
    for blk in range(nblk):
        r0 = blk * A_BLK
        has_prev = jnp.logical_not(first) if blk == 0 else None
        scores, pps, scale_terms, rs = {}, {}, {}, {}
        for g, par in bodies:
            c0 = g * A_GROUP * A_HD
            qst = jnp.concatenate([q_s[r0:r0 + A_BLK, c0:c0 + LANES],
                                   q_s[r0:r0 + A_BLK, c0 + LANES:c0 + 2 * LANES]], axis=0)
            kb = bufs[par][0][r0:r0 + 2 * A_BLK, g * LANES:(g + 1) * LANES]
            scores[g, par] = _dot_nt(qst, kb)
        for g, par in bodies:
            s = scores[g, par]
            s_prev = s[:, :A_BLK]
            if has_prev is not None:
                s_prev = jnp.where(has_prev, s_prev, NEG_BIG)
            sc = jnp.where(causal, s[:, A_BLK:], s_prev)
            sink = jnp.concatenate(
                [jnp.full((A_BLK, 1), sink_ref[g * A_GROUP + 2 * pr + par], F32)
                 for pr in range(2)], axis=0)
            m = jnp.maximum(jnp.max(sc, axis=-1, keepdims=True), sink)
            mb = jnp.broadcast_to(m, sc.shape)
            p = jnp.exp(sc - mb)
            pps[g, par] = jnp.concatenate([jnp.where(causal, 0.0, p), jnp.where(causal, p, 0.0)],
                                          axis=1).astype(BF16)
            scale_terms[g, par] = jnp.exp(jnp.broadcast_to(sink, sc.shape) - mb)
        for g, par in bodies:
            vb = bufs[par][1][r0:r0 + 2 * A_BLK, g * LANES:(g + 1) * LANES]
            rs[g, par] = _dot(pps[g, par], jnp.concatenate([vb, ones], axis=1))
        for g in range(A_KV):
            c0 = g * A_GROUP * A_HD
            acc = None
            for par in range(2):
                r = rs[g, par]
                o = r[:, :LANES] * (1.0 / (r[:, LANES:] + scale_terms[g, par]))
                acc = o if acc is None else acc + o
            o_s[r0:r0 + A_BLK, c0:c0 + LANES] = acc[:A_BLK].astype(BF16)
            o_s[r0:r0 + A_BLK, c0 + LANES:c0 + 2 * LANES] = acc[A_BLK:].astype(BF16)

    out_ref[...] = x + _dot(o_s[...], wout_ref[...])


def _attn_layer(h, gain, w_in, w_out, cosf, sins, qg, kg, sinks, *, layer, seq, tile):
    m = h.shape[0]
    tiles_per_seq = seq // tile
    kern = functools.partial(_attn_kernel, tile=tile, tiles_per_seq=tiles_per_seq)
    row = lambda i: (i, 0)
    fixed = lambda i: (0, 0)
    return pl.pallas_call(
        kern,
        grid=(m // tile,),
        in_specs=[
            pl.BlockSpec(memory_space=pltpu.SMEM),
            pl.BlockSpec((tile, D_MODEL), row),
            pl.BlockSpec((1, D_MODEL), fixed),
            _resident((D_MODEL, A_Q + 2 * A_KVD), layer),
            _resident((A_Q, D_MODEL), layer),
            pl.BlockSpec((tile, LANES), row),
            pl.BlockSpec((tile, LANES), row),
            pl.BlockSpec((1, LANES), fixed),
            pl.BlockSpec((1, LANES), fixed),
        ],
        out_specs=pl.BlockSpec((tile, D_MODEL), row),
        out_shape=jax.ShapeDtypeStruct((m, D_MODEL), F32),
        scratch_shapes=[
            pltpu.VMEM((tile, A_Q), BF16),
            pltpu.VMEM((tile + A_BLK, A_KV * LANES), BF16),
            pltpu.VMEM((tile + A_BLK, A_KV * LANES), BF16),
            pltpu.VMEM((tile + A_BLK, A_KV * LANES), BF16),
            pltpu.VMEM((tile + A_BLK, A_KV * LANES), BF16),
            pltpu.VMEM((tile, A_Q), BF16),
        ],
        compiler_params=pltpu.CompilerParams(dimension_semantics=("arbitrary",),
                                             vmem_limit_bytes=VMEM_LIMIT),
        name="attn_layer",
    )(sinks, h, gain, w_in, w_out, cosf, sins, qg, kg)


CONV_ROWS = 256
LOOKAHEAD = 2


def _conv_stage(vals, ext, first):
    tm = vals.shape[0]
    ext[0:SUBLANES, :] = jnp.where(first, 0.0, ext[tm:tm + SUBLANES, :])
    ext[SUBLANES:SUBLANES + tm, :] = vals


def _conv_block(ext, r, cw, taps):
    xe = ext[r:r + SUBLANES + CONV_ROWS, :]
    if taps == 4:
        xs = pltpu.roll(xe, 1, 0)
        older = xs * cw[0:1, :] + xe * cw[1:2, :]
        newer = xs * cw[2:3, :] + xe * cw[3:4, :]
        return (pltpu.roll(older, 2, 0) + newer)[SUBLANES:]
    z = xe * cw[0:1, :]
    for k in range(1, taps):
        z = pltpu.roll(z, 1, 0) + xe * cw[k:k + 1, :]
    return z[SUBLANES:]


def _resident(shape, layer=None):
    if layer is None:
        return pl.BlockSpec(shape, lambda *_: (0,) * len(shape), pipeline_mode=pl.Buffered(1))
    return pl.BlockSpec((None,) + tuple(shape), lambda *_: (layer,) + (0,) * len(shape),
                        pipeline_mode=pl.Buffered(1))


def _ffn_kernel(*refs, tiles_per_seq, tf, down_every, mixer_out):
    if mixer_out:
        mix_ref, wmix_ref, *refs = refs
    h_ref, gain_ref, wup_ref, cw_ref, cb_ref, wd_ref, out_ref, ext_s, up_s, act_s = refs
    i = pl.program_id(0)
    first = (i % tiles_per_seq) == 0
    tm = h_ref.shape[0]
    nj = D_FF // tf

    @pl.when(i == 0)
    def _():
        ext_s[:, tm:tm + SUBLANES, :] = jnp.zeros((nj, SUBLANES, tf), F32)

    x = h_ref[...]
    if mixer_out:
        x = x + _dot(mix_ref[...], wmix_ref[...])
    xn = _rms(x, gain_ref[...]).astype(BF16)

    def produce(j):
        _conv_stage(_dot(xn, wup_ref[:, j * tf:(j + 1) * tf]), ext_s.at[j], first)
        up_s[j] = _dot(xn, wup_ref[:, D_FF + j * tf:D_FF + (j + 1) * tf])

    acc = x
    for j in range(min(LOOKAHEAD, nj)):
        produce(j)
    for j in range(nj):
        if j + LOOKAHEAD < nj:
            produce(j + LOOKAHEAD)
        cols = slice(j * tf, (j + 1) * tf)
        cw = cw_ref[:, cols]
        cb = cb_ref[:, cols]
        for r in range(0, tm, CONV_ROWS):
            conv = _conv_block(ext_s.at[j], r, cw, F_CONV) + cb
            act_s[r:r + CONV_ROWS, cols] = (_silu(conv) * up_s[j, r:r + CONV_ROWS, :]).astype(BF16)
        if (j + 1) % down_every == 0:
            kc = slice((j + 1 - down_every) * tf, (j + 1) * tf)
            acc = acc + _dot(act_s[:, kc], wd_ref[kc, :])
    out_ref[...] = acc


def _ffn_layer(h, gain, w_up, conv_w, conv_b, w_down, *, layer, seq, tm, tf, mixer=None):
    m = h.shape[0]
    nj = D_FF // tf
    kern = functools.partial(_ffn_kernel, tiles_per_seq=seq // tm, tf=tf,
                             down_every=D_MODEL // tf, mixer_out=mixer is not None)
    mixer_specs, mixer_args = [], []
    if mixer is not None:
        mix, w_mix, mix_layer = mixer
        kdim = mix.shape[1]
        mixer_specs = [pl.BlockSpec((tm, kdim), lambda i: (i, 0)),
                       _resident((kdim, D_MODEL), mix_layer)]
        mixer_args = [mix, w_mix]
    return pl.pallas_call(
        kern,
        grid=(m // tm,),
        in_specs=mixer_specs + [
            pl.BlockSpec((tm, D_MODEL), lambda i: (i, 0)),
            _resident((1, D_MODEL)),
            _resident((D_MODEL, 2 * D_FF), layer),
            _resident((F_CONV, D_FF)),
            _resident((1, D_FF)),
            _resident((D_FF, D_MODEL), layer),
        ],
        out_specs=pl.BlockSpec((tm, D_MODEL), lambda i: (i, 0)),
        out_shape=jax.ShapeDtypeStruct((m, D_MODEL), F32),
        scratch_shapes=[pltpu.VMEM((nj, tm + SUBLANES, tf), F32),
                        pltpu.VMEM((nj, tm, tf), F32),
                        pltpu.VMEM((tm, D_FF), BF16)],
        compiler_params=pltpu.CompilerParams(dimension_semantics=("arbitrary",),
                                             vmem_limit_bytes=VMEM_LIMIT),
        name="ffn_layer",
    )(*mixer_args, h, gain, w_up, conv_w, conv_b, w_down)


def _gdn_in_kernel(h_ref, gain_ref, w_ref, wba_ref, cw_ref, alog_ref, dtb_ref,
                   qkv_ref, z_ref, bg_ref, ext_s, *, tiles_per_seq, tn):
    i = pl.program_id(0)
    first = (i % tiles_per_seq) == 0
    tm = h_ref.shape[0]
    n_conv = G_QKV // tn
    n_all = G_MAIN // tn

    @pl.when(i == 0)
    def _():
        ext_s[:, tm:tm + SUBLANES, :] = jnp.zeros((n_conv, SUBLANES, tn), F32)

    xn = _rms(h_ref[...], gain_ref[...]).astype(BF16)
    ba = _dot(xn, wba_ref[...])
    lane = lax.broadcasted_iota(jnp.int32, ba.shape, 1)
    beta = 1.0 / (1.0 + jnp.exp(-ba))
    sp_in = ba + dtb_ref[...]
    softplus = jnp.maximum(sp_in, 0.0) + jnp.log1p(jnp.exp(-jnp.abs(sp_in)))
    g = -jnp.exp(alog_ref[...]) * softplus
    bg_ref[...] = jnp.where(lane < G_V, beta, jnp.where(lane < 2 * G_V, g, 0.0))

    def produce(j):
        cols = slice(j * tn, (j + 1) * tn)
        proj = _dot(xn, w_ref[:, cols])
        if j < n_conv:
            _conv_stage(proj, ext_s.at[j], first)
        else:
            z_ref[:, (j - n_conv) * tn:(j - n_conv + 1) * tn] = proj.astype(BF16)

    n_z = n_all - n_conv
    per_z = n_conv // n_z
    order = []
    for zc in range(n_z):
        order += list(range(zc * per_z, (zc + 1) * per_z)) + [n_conv + zc]
    for j in order[:LOOKAHEAD]:
        produce(j)
    for pos, j in enumerate(order):
        if pos + LOOKAHEAD < n_all:
            produce(order[pos + LOOKAHEAD])
        if j < n_conv:
            cols = slice(j * tn, (j + 1) * tn)
            cw = cw_ref[:, cols]
            for r in range(0, tm, CONV_ROWS):
                qkv_ref[r:r + CONV_ROWS, cols] = _silu(
                    _conv_block(ext_s.at[j], r, cw, G_CONV)).astype(BF16)


def _gdn_in_proj(h, gain, w_main, w_ba, conv_w, alog, dtb, *, layer, seq, tm, tn):
    m = h.shape[0]
    kern = functools.partial(_gdn_in_kernel, tiles_per_seq=seq // tm, tn=tn)
    cur_tile = lambda i: (i, 0)
    prev_tile = cur_tile
    return pl.pallas_call(
        kern,
        grid=(m // tm,),
        in_specs=[
            pl.BlockSpec((tm, D_MODEL), cur_tile),
            _resident((1, D_MODEL)),
            _resident((D_MODEL, G_MAIN), layer),
            _resident((D_MODEL, LANES)),
            _resident((G_CONV, G_QKV)),
            _resident((1, LANES)),
            _resident((1, LANES)),
        ],
        out_specs=[
            pl.BlockSpec((tm, G_QKV), prev_tile),
            pl.BlockSpec((tm, G_VAL), cur_tile),
            pl.BlockSpec((tm, LANES), cur_tile),
        ],
        out_shape=[
            jax.ShapeDtypeStruct((m, G_QKV), BF16),
            jax.ShapeDtypeStruct((m, G_VAL), BF16),
            jax.ShapeDtypeStruct((m, LANES), F32),
        ],
        scratch_shapes=[pltpu.VMEM((G_QKV // tn, tm + SUBLANES, tn), F32)],
        compiler_params=pltpu.CompilerParams(dimension_semantics=("arbitrary",),
                                             vmem_limit_bytes=VMEM_LIMIT),
        name="gdn_in_proj",
    )(h, gain, w_main, w_ba, conv_w, alog, dtb)


G_PAIR = G_V // G_QK


G_WIDE = 256
G_CPB = G_WIDE // G_CHUNK


def _to_wide(full, blk):
    out = full[0:G_CHUNK]
    for cc in range(1, G_CPB):
        out = jnp.where(blk == cc, full[cc * G_CHUNK:(cc + 1) * G_CHUNK], out)
    return out


def _col_to_wide(col, blk):
    out = col[0:G_CHUNK]
    for cc in range(1, G_CPB):
        out = jnp.where(blk == cc, col[cc * G_CHUNK:(cc + 1) * G_CHUNK], out)
    return out


def _block_diag(wide, blk):
    zero = jnp.zeros_like(wide)
    return jnp.concatenate([jnp.where(blk == cc, wide, zero) for cc in range(G_CPB)], axis=0)


def _gdn_chunk_kernel(q_ref, k_ref, v_ref, z_ref, bg_ref, bgt_ref, ng_ref, out_ref, st_s, rows_s,
                      *, tc, heads):
    hg = pl.program_id(1)
    c = pl.program_id(2)
    nblk = tc // G_WIDE

    @pl.when(c == 0)
    def _():
        st_s[...] = jnp.zeros(st_s.shape, F32)

    ri = lax.broadcasted_iota(jnp.int32, (G_WIDE, G_WIDE), 0)
    ci = lax.broadcasted_iota(jnp.int32, (G_WIDE, G_WIDE), 1)
    same = (ri // G_CHUNK) == (ci // G_CHUNK)
    ones_incl = jnp.where(jnp.logical_and(same, ci <= ri), 1.0, 0.0).astype(BF16)
    ones_inclT = jnp.where(jnp.logical_and(same, ri <= ci), 1.0, 0.0).astype(BF16)
    ones_same = jnp.where(same, 1.0, 0.0).astype(BF16)

    wi = lax.broadcasted_iota(jnp.int32, (G_CHUNK, G_WIDE), 0)
    wl = lax.broadcasted_iota(jnp.int32, (G_CHUNK, G_WIDE), 1)
    wblk = wl // G_CHUNK
    wj = wl % G_CHUNK
    incl_w = wj <= wi
    strict_w = wj < wi
    eye_w = jnp.where(wj == wi, 1.0, 0.0)
    kblk = lax.broadcasted_iota(jnp.int32, (G_DK, G_WIDE), 1) // G_CHUNK

    bg = bg_ref[...]
    bgt = bgt_ref[...]
    gc_cols, gc_rows, gt_rows = [], [], []
    for b in range(nblk):
        sl = slice(b * G_WIDE, (b + 1) * G_WIDE)
        gcc = jnp.zeros((G_WIDE, LANES), F32)
        for piece in _split3(bg[sl]):
            gcc = gcc + _dot(ones_incl, piece)
        gc_cols.append(gcc)
        gcr = jnp.zeros((2 * G_V, G_WIDE), F32)
        gtr = jnp.zeros((2 * G_V, G_WIDE), F32)
        for piece in _split3(bgt[:, sl]):
            gcr = gcr + _dot(piece, ones_inclT)
            gtr = gtr + _dot(piece, ones_same)
        gc_rows.append(gcr)
        gt_rows.append(gtr)
    gc_all = jnp.concatenate(gc_cols, axis=0)
    rows_s[0:2 * G_V, :] = bgt
    rows_s[2 * G_V:4 * G_V, :] = jnp.concatenate(gc_rows, axis=1)
    rows_s[4 * G_V:6 * G_V, :] = jnp.concatenate(gt_rows, axis=1)

    lane = lax.broadcasted_iota(jnp.int32, (tc, LANES), 1)
    ngain = ng_ref[...]

    qk_heads = []
    for qh in range(heads // G_PAIR):
        q = q_ref[:, qh * G_DK:(qh + 1) * G_DK].astype(F32)
        k = k_ref[:, qh * G_DK:(qh + 1) * G_DK].astype(F32)
        qn = q * lax.rsqrt(jnp.sum(q * q, axis=-1, keepdims=True) + EPS) * (G_DK ** -0.5)
        kn = k * lax.rsqrt(jnp.sum(k * k, axis=-1, keepdims=True) + EPS)
        qb = qn.astype(BF16)
        kb = kn.astype(BF16)
        kn_t = kn.T
        kk_w, qk_w = [], []
        for b in range(nblk):
            sl = slice(b * G_WIDE, (b + 1) * G_WIDE)
            kk_w.append(_to_wide(_dot_nt(kb[sl], kb[sl]), wblk))
            qk_w.append(_to_wide(_dot_nt(qb[sl], kb[sl]), wblk))
        qk_heads.append((qn, kn, kn_t, kk_w, qk_w))

    per_head = {}
    chains = {}

    def st_prep(hs):
        for hh in hs:
            head = hg * heads + hh
            qn, kn, kn_t, kk_w, qk_w = qk_heads[hh // G_PAIR]
            pick = lambda a, off: jnp.sum(jnp.where(lane == head + off, a, 0.0), axis=-1,
                                          keepdims=True)
            beta_c = pick(bg, 0)
            gc_c = pick(gc_all, G_V)
            beta_r = rows_s[pl.ds(head, 1), :]
            gc_r = rows_s[pl.ds(3 * G_V + head, 1), :]
            gt_r = rows_s[pl.ds(5 * G_V + head, 1), :]
            e_gc_c = jnp.exp(gc_c)
            vk = jnp.concatenate([v_ref[:, hh * G_DV:(hh + 1) * G_DV],
                                  (kn * e_gc_c).astype(BF16)], axis=1)
            per_head[hh] = dict(beta_r=beta_r, e_gt_r=jnp.exp(gt_r),
                                e_rest_r=jnp.exp(gt_r - gc_r), qd=qn * e_gc_c, vk=vk, kn_t=kn_t)
            for b in range(nblk):
                sl = slice(b * G_WIDE, (b + 1) * G_WIDE)
                decay_w = jnp.exp(jnp.where(incl_w, _col_to_wide(gc_c[sl], wblk) - gc_r[:, sl],
                                            NEG_BIG))
                n_w = jnp.where(strict_w,
                                -(_col_to_wide(beta_c[sl], wblk) * kk_w[b] * decay_w), 0.0)
                qkm = jnp.where(incl_w, qk_w[b] * decay_w, 0.0).astype(BF16)
                chains[hh, b] = dict(hh=hh, sl=sl, n_w=n_w, qkm=qkm)

    def group_chains(hs):
        return [chains[hh, b] for hh in hs for b in range(nblk)]

    def level_mask(s):
        return jnp.logical_and(wi // (2 * s) == wj // (2 * s),
                               jnp.logical_and(wi % (2 * s) >= s, wj % (2 * s) < s))

    def st_first(hs):
        for ch in group_chains(hs):
            ch["t_w"] = eye_w + jnp.where(level_mask(1), ch["n_w"], 0.0)

    def st_join(s):
        def stage(hs):
            for ch in group_chains(hs):
                c_neg = jnp.where(level_mask(s), ch["n_w"], 0.0).astype(BF16)
                ch["x_w"] = _dot(ch["t_w"].astype(BF16), _block_diag(c_neg, wblk))
        return stage

    def st_merge(hs):
        for ch in group_chains(hs):
            ch["t_w"] = ch["t_w"] + _dot(ch["x_w"].astype(BF16),
                                         _block_diag(ch["t_w"].astype(BF16), wblk))

    def st_uw(hs):
        for ch in group_chains(hs):
            ph, sl = per_head[ch["hh"]], ch["sl"]
            t_beta = (ch["t_w"] * ph["beta_r"][:, sl]).astype(BF16)
            ch["uw"] = _dot(_block_diag(t_beta, wblk), ph["vk"][sl]).astype(BF16)

    def st_big(hs):
        for ch in group_chains(hs):
            ph, sl = per_head[ch["hh"]], ch["sl"]
            kd_t = (ph["kn_t"][:, sl] * ph["e_rest_r"][:, sl]).astype(BF16)
            zero = jnp.zeros_like(kd_t)
            lhs = jnp.concatenate([_block_diag(ch["qkm"], wblk)]
                                  + [jnp.where(kblk == cc, kd_t, zero) for cc in range(G_CPB)],
                                  axis=0)
            big = _dot(lhs, ch["uw"])
            ch["qp"] = (ph["qd"][sl] - big[:G_WIDE, G_DV:]).astype(BF16)
            ch["qu"] = big[:G_WIDE, :G_DV]
            ch["wk"] = [big[G_WIDE + cc * G_DK:G_WIDE + (cc + 1) * G_DK, G_DV:].astype(BF16)
                        for cc in range(G_CPB)]
            ch["uk"] = [big[G_WIDE + cc * G_DK:G_WIDE + (cc + 1) * G_DK, :G_DV]
                        for cc in range(G_CPB)]

    def st_recur(hs):
        states = {hh: st_s[hh] for hh in hs}
        outs = {hh: [] for hh in hs}
        for b in range(nblk):
            for cc in range(G_CPB):
                rows = slice(cc * G_CHUNK, (cc + 1) * G_CHUNK)
                t0 = b * G_WIDE + cc * G_CHUNK
                for hh in hs:
                    ch = chains[hh, b]
                    xs = _dot(jnp.concatenate([ch["wk"][cc], ch["qp"][rows]], axis=0),
                              states[hh].astype(BF16))
                    outs[hh].append(xs[G_DK:] + ch["qu"][rows])
                    states[hh] = (states[hh] * per_head[hh]["e_gt_r"][:, t0:t0 + 1]
                                  - xs[:G_DK] + ch["uk"][cc])
        for hh in hs:
            st_s[hh] = states[hh]
            o = jnp.concatenate(outs[hh], axis=0)
            z = z_ref[:, hh * G_DV:(hh + 1) * G_DV].astype(F32)
            y = _rms(o, ngain) * _silu(z)
            out_ref[:, hh * G_DV:(hh + 1) * G_DV] = y.astype(BF16)

    solve = []
    s = 2
    while s < G_CHUNK:
        solve += [st_join(s), st_merge]
        s *= 2
    for stage in [st_prep, st_first] + solve + [st_uw, st_big, st_recur]:
        stage(list(range(heads)))


def _gdn_chunk(qkv, z, bg, bgt, ngain, *, batch, seq, tc, heads):
    m = qkv.shape[0]
    nc = seq // tc
    kern = functools.partial(_gdn_chunk_kernel, tc=tc, heads=heads)
    qk_w = heads // G_PAIR * G_DK
    v_w = heads * G_DV
    rowi = lambda b, g, c: b * nc + c
    return pl.pallas_call(
        kern,
        grid=(batch, G_V // heads, nc),
        in_specs=[
            pl.BlockSpec((tc, qk_w), lambda b, g, c: (rowi(b, g, c), g)),
            pl.BlockSpec((tc, qk_w), lambda b, g, c: (rowi(b, g, c), G_KEY // qk_w + g)),
            pl.BlockSpec((tc, v_w), lambda b, g, c: (rowi(b, g, c), 2 * G_KEY // v_w + g)),
            pl.BlockSpec((tc, v_w), lambda b, g, c: (rowi(b, g, c), g)),
            pl.BlockSpec((tc, LANES), lambda b, g, c: (rowi(b, g, c), 0)),
            pl.BlockSpec((None, 2 * G_V, tc), lambda b, g, c: (b, 0, c)),
            pl.BlockSpec((1, G_DV), lambda b, g, c: (0, 0)),
        ],
        out_specs=pl.BlockSpec((tc, v_w), lambda b, g, c: (rowi(b, g, c), g)),
        out_shape=jax.ShapeDtypeStruct((m, G_VAL), BF16),
        scratch_shapes=[pltpu.VMEM((heads, G_DK, G_DV), F32),
                        pltpu.VMEM((6 * G_V, tc), F32)],
        compiler_params=pltpu.CompilerParams(
            dimension_semantics=("arbitrary", "arbitrary", "arbitrary"),
            vmem_limit_bytes=VMEM_LIMIT),
        name="gdn_chunk",
    )(qkv, qkv, qkv, z, bg, bgt, ngain)


def _rope_tables(positions):
    half = ROPE_DIM // 2
    inv_freq = ROPE_THETA ** (-jnp.arange(0, ROPE_DIM, 2, dtype=F32) / ROPE_DIM)
    ang = positions.astype(F32).reshape(-1, 1) * inv_freq
    cos, sin = jnp.cos(ang), jnp.sin(ang)
    n = ang.shape[0]
    cos_h = jnp.concatenate([cos, cos, jnp.ones((n, A_HD - 2 * half), F32)], axis=1)
    sin_h = jnp.concatenate([-sin, sin, jnp.zeros((n, A_HD - 2 * half), F32)], axis=1)
    return jnp.tile(cos_h, (1, LANES // A_HD)), jnp.tile(sin_h, (1, LANES // A_HD))


def kernel(x, positions, mixer_norm, ffn_norm, attn_w_in, attn_q_gain, attn_k_gain, attn_sinks,
           attn_w_out, gdn_w_in, gdn_conv, gdn_a_log, gdn_dt_bias, gdn_norm, gdn_w_out,
           ffn_w_up, ffn_conv, ffn_conv_b, ffn_w_down):
    batch, seq, _ = x.shape
    depth = mixer_norm.shape[0]
    m = batch * seq
    attn_tile = min(1024, seq)
    row_tile = min(512, seq)
    gdn_tc = min(512, seq)

    cosf, sins = _rope_tables(positions)
    attn_w_in_b = attn_w_in.astype(BF16)
    attn_w_out_b = attn_w_out.astype(BF16)
    gdn_w_main_b = gdn_w_in.astype(BF16)
    gdn_w_out_b = gdn_w_out.astype(BF16)
    ffn_w_up_b = ffn_w_up.astype(BF16)
    ffn_w_down_b = ffn_w_down.astype(BF16)

    h = x.reshape(m, D_MODEL)
    for i in range(depth):
        j = i // 2
        gain = mixer_norm[i].reshape(1, D_MODEL)
        mixer = None
        if i % 2 == 0:
            qg = jnp.tile(attn_q_gain[j].reshape(1, A_HD), (1, LANES // A_HD)) * (A_HD ** -0.5)
            kg = jnp.tile(attn_k_gain[j].reshape(1, A_HD), (1, LANES // A_HD))
            h = _attn_layer(h, gain, attn_w_in_b, attn_w_out_b, cosf, sins, qg, kg,
                            attn_sinks[j].astype(F32), layer=j, seq=seq, tile=attn_tile)
        else:
            w_ba = jnp.pad(gdn_w_in[j][:, G_MAIN:], ((0, 0), (0, LANES - 2 * G_V))).astype(BF16)
            pad16 = lambda a: jnp.pad(a.astype(F32).reshape(1, G_V), ((0, 0), (G_V, LANES - 2 * G_V)))
            qkv, z, bg = _gdn_in_proj(h, gain, gdn_w_main_b, w_ba, gdn_conv[j],
                                      pad16(gdn_a_log[j]), pad16(gdn_dt_bias[j]),
                                      layer=j, seq=seq, tm=row_tile, tn=256)
            bgt = jnp.transpose(bg.reshape(batch, seq, LANES)[:, :, :2 * G_V], (0, 2, 1))
            o = _gdn_chunk(qkv, z, bg, bgt, gdn_norm[j].reshape(1, G_DV), batch=batch, seq=seq,
                           tc=gdn_tc, heads=8)
            mixer = (o, gdn_w_out_b, j)
        h = _ffn_layer(h, ffn_norm[i].reshape(1, D_MODEL), ffn_w_up_b, ffn_conv[i],
                       ffn_conv_b[i].reshape(1, D_FF), ffn_w_down_b,
                       layer=i, seq=seq, tm=row_tile, tf=256, mixer=mixer)
    return h.reshape(batch, seq, D_MODEL)
```

```python
import functools

import jax
import jax.numpy as jnp
from jax import lax
from jax.experimental import pallas as pl
from jax.experimental.pallas import tpu as pltpu

F32 = jnp.float32
BF16 = jnp.bfloat16
EPS = 1e-6

D_MODEL = 1024
A_HEADS = 16
A_KV = 4
A_GROUP = A_HEADS // A_KV
A_HD = 64
A_Q = A_HEADS * A_HD
A_KVD = A_KV * A_HD
A_BLK = 128
ROPE_DIM = A_HD // 4
ROPE_THETA = 500000.0
G_QK = 8
G_V = 16
G_DK = 128
G_DV = 128
G_CONV = 4
G_CHUNK = 64
G_KEY = G_QK * G_DK
G_VAL = G_V * G_DV
G_QKV = 2 * G_KEY + G_VAL
G_MAIN = G_QKV + G_VAL
D_FF = 3 * D_MODEL
F_CONV = 3

LANES = 128
SUBLANES = 8
VMEM_LIMIT = 56 * 1024 * 1024
NEG_BIG = -1e30


def _dot(a, b):
    return jnp.dot(a, b, preferred_element_type=F32)


def _dot_nt(a, b):
    return lax.dot_general(a, b, (((1,), (1,)), ((), ())), preferred_element_type=F32)


def _rms(x, gain):
    ms = jnp.mean(x * x, axis=-1, keepdims=True)
    return x * lax.rsqrt(ms + EPS) * gain


def _silu(x):
    hx = 0.5 * x
    return hx * (1.0 + jnp.tanh(hx))


def _split3(x):
    a = x.astype(BF16)
    r = x - a.astype(F32)
    b = r.astype(BF16)
    c = (r - b.astype(F32)).astype(BF16)
    return a, b, c


def _norm_rope_pair(x, gain, cosf, sins, lane):
    x2 = x * x
    lo = lane < A_HD
    s_lo = jnp.sum(jnp.where(lo, x2, 0.0), axis=-1, keepdims=True)
    s_hi = jnp.sum(jnp.where(lo, 0.0, x2), axis=-1, keepdims=True)
    ms = jnp.where(lo, s_lo, s_hi) * (1.0 / A_HD)
    y = x * lax.rsqrt(ms + EPS) * gain
    rot = jnp.where((lane % A_HD) < ROPE_DIM // 2,
                    pltpu.roll(y, LANES - ROPE_DIM // 2, 1), pltpu.roll(y, ROPE_DIM // 2, 1))
    return y * cosf + rot * sins


def _attn_kernel(sink_ref, h_ref, gain_ref, win_ref, wout_ref, cos_ref, sin_ref, qg_ref, kg_ref,
                 out_ref, q_s, klo_s, khi_s, vlo_s, vhi_s, o_s, *, tile, tiles_per_seq):
    step = pl.program_id(0)
    first = (step % tiles_per_seq) == 0
    nblk = tile // A_BLK
    kv_bufs = (klo_s, khi_s, vlo_s, vhi_s)

    @pl.when(first)
    def _():
        for buf in kv_bufs:
            buf[0:A_BLK, :] = jnp.zeros((A_BLK, A_KV * LANES), BF16)

    @pl.when(jnp.logical_not(first))
    def _():
        for buf in kv_bufs:
            buf[0:A_BLK, :] = buf[tile:tile + A_BLK, :]

    x = h_ref[...]
    xn = _rms(x, gain_ref[...]).astype(BF16)
    qkv = _dot(xn, win_ref[...])

    cosf = cos_ref[...]
    sins = sin_ref[...]
    lane = lax.broadcasted_iota(jnp.int32, (tile, LANES), 1)
    low = lane < A_HD
    qg = qg_ref[...]
    kg = kg_ref[...]
    for p in range(A_Q // LANES):
        c0 = p * LANES
        qp = _norm_rope_pair(qkv[:, c0:c0 + LANES], qg, cosf, sins, lane)
        q_s[:, c0:c0 + LANES] = qp.astype(BF16)

    def place(pair, lo_s, hi_s, p):
        swapped = pltpu.roll(pair, A_HD, 1)
        rows = slice(A_BLK, A_BLK + tile)
        g0, g1 = 2 * p, 2 * p + 1
        lo_s[rows, g0 * LANES:(g0 + 1) * LANES] = jnp.where(low, pair, 0.0).astype(BF16)
        hi_s[rows, g0 * LANES:(g0 + 1) * LANES] = jnp.where(low, 0.0, swapped).astype(BF16)
        lo_s[rows, g1 * LANES:(g1 + 1) * LANES] = jnp.where(low, swapped, 0.0).astype(BF16)
        hi_s[rows, g1 * LANES:(g1 + 1) * LANES] = jnp.where(low, 0.0, pair).astype(BF16)

    for p in range(A_KVD // LANES):
        c0 = A_Q + p * LANES
        place(_norm_rope_pair(qkv[:, c0:c0 + LANES], kg, cosf, sins, lane), klo_s, khi_s, p)
        place(qkv[:, c0 + A_KVD:c0 + A_KVD + LANES], vlo_s, vhi_s, p)

    rows2 = 2 * A_BLK
    qi = lax.broadcasted_iota(jnp.int32, (rows2, A_BLK), 0) % A_BLK
    kj = lax.broadcasted_iota(jnp.int32, (rows2, A_BLK), 1)
    causal = kj <= qi
    ones = jnp.ones((2 * A_BLK, LANES), BF16)

    bufs = ((klo_s, vlo_s), (khi_s, vhi_s))
    bodies = [(g, par) for g in range(A_KV) for par in range(2)]
    for blk in range(nblk):
        r0 = blk * A_BLK
        has_prev = jnp.logical_not(first) if blk == 0 else None
        scores, pps, scale_terms, rs = {}, {}, {}, {}
        for g, par in bodies:
            c0 = g * A_GROUP * A_HD
            qst = jnp.concatenate([q_s[r0:r0 + A_BLK, c0:c0 + LANES],
                                   q_s[r0:r0 + A_BLK, c0 + LANES:c0 + 2 * LANES]], axis=0)
            kb = bufs[par][0][r0:r0 + 2 * A_BLK, g * LANES:(g + 1) * LANES]
            scores[g, par] = _dot_nt(qst, kb)
        for g, par in bodies:
            s = scores[g, par]
            s_prev = s[:, :A_BLK]
            if has_prev is not None:
                s_prev = jnp.where(has_prev, s_prev, NEG_BIG)
            sc = jnp.where(causal, s[:, A_BLK:], s_prev)
            sink = jnp.concatenate(
                [jnp.full((A_BLK, 1), sink_ref[g * A_GROUP + 2 * pr + par], F32)
                 for pr in range(2)], axis=0)
            m = jnp.maximum(jnp.max(sc, axis=-1, keepdims=True), sink)
            mb = jnp.broadcast_to(m, sc.shape)
            p = jnp.exp(sc - mb)
            pps[g, par] = jnp.concatenate([jnp.where(causal, 0.0, p), jnp.where(causal, p, 0.0)],
                                          axis=1).astype(BF16)
            scale_terms[g, par] = jnp.exp(jnp.broadcast_to(sink, sc.shape) - mb)
        for g, par in bodies:
            vb = bufs[par][1][r0:r0 + 2 * A_BLK, g * LANES:(g + 1) * LANES]
            rs[g, par] = _dot(pps[g, par], jnp.concatenate([vb, ones], axis=1))
        for g in range(A_KV):
            c0 = g * A_GROUP * A_HD
            acc = None
            for par in range(2):
                r = rs[g, par]
                o = r[:, :LANES] * (1.0 / (r[:, LANES:] + scale_terms[g, par]))
                acc = o if acc is None else acc + o
            o_s[r0:r0 + A_BLK, c0:c0 + LANES] = acc[:A_BLK].astype(BF16)
            o_s[r0:r0 + A_BLK, c0 + LANES:c0 + 2 * LANES] = acc[A_BLK:].astype(BF16)

    out_ref[...] = x + _dot(o_s[...], wout_ref[...])


def _attn_layer(h, gain, w_in, w_out, cosf, sins, qg, kg, sinks, *, layer, seq, tile):
    m = h.shape[0]
    tiles_per_seq = seq // tile
    kern = functools.partial(_attn_kernel, tile=tile, tiles_per_seq=tiles_per_seq)
    row = lambda i: (i, 0)
    fixed = lambda i: (0, 0)
    return pl.pallas_call(
        kern,
        grid=(m // tile,),
        in_specs=[
            pl.BlockSpec(memory_space=pltpu.SMEM),
            pl.BlockSpec((tile, D_MODEL), row),
            pl.BlockSpec((1, D_MODEL), fixed),
            _resident((D_MODEL, A_Q + 2 * A_KVD), layer),
            _resident((A_Q, D_MODEL), layer),
            pl.BlockSpec((tile, LANES), row),
            pl.BlockSpec((tile, LANES), row),
            pl.BlockSpec((1, LANES), fixed),
            pl.BlockSpec((1, LANES), fixed),
        ],
        out_specs=pl.BlockSpec((tile, D_MODEL), row),
        out_shape=jax.ShapeDtypeStruct((m, D_MODEL), F32),
        scratch_shapes=[
            pltpu.VMEM((tile, A_Q), BF16),
            pltpu.VMEM((tile + A_BLK, A_KV * LANES), BF16),
            pltpu.VMEM((tile + A_BLK, A_KV * LANES), BF16),
            pltpu.VMEM((tile + A_BLK, A_KV * LANES), BF16),
            pltpu.VMEM((tile + A_BLK, A_KV * LANES), BF16),
            pltpu.VMEM((tile, A_Q), BF16),
        ],
        compiler_params=pltpu.CompilerParams(dimension_semantics=("arbitrary",),
                                             vmem_limit_bytes=VMEM_LIMIT),
        name="attn_layer",
    )(sinks, h, gain, w_in, w_out, cosf, sins, qg, kg)


CONV_ROWS = 256
LOOKAHEAD = 2


def _conv_stage(vals, ext, first):
    tm = vals.shape[0]
    ext[0:SUBLANES, :] = jnp.where(first, 0.0, ext[tm:tm + SUBLANES, :])
    ext[SUBLANES:SUBLANES + tm, :] = vals


def _conv_block(ext, r, cw, taps):
    xe = ext[r:r + SUBLANES + CONV_ROWS, :]
    if taps == 4:
        xs = pltpu.roll(xe, 1, 0)
        older = xs * cw[0:1, :] + xe * cw[1:2, :]
        newer = xs * cw[2:3, :] + xe * cw[3:4, :]
        return (pltpu.roll(older, 2, 0) + newer)[SUBLANES:]
    z = xe * cw[0:1, :]
    for k in range(1, taps):
        z = pltpu.roll(z, 1, 0) + xe * cw[k:k + 1, :]
    return z[SUBLANES:]


def _resident(shape, layer=None):
    if layer is None:
        return pl.BlockSpec(shape, lambda *_: (0,) * len(shape), pipeline_mode=pl.Buffered(1))
    return pl.BlockSpec((None,) + tuple(shape), lambda *_: (layer,) + (0,) * len(shape),
                        pipeline_mode=pl.Buffered(1))


def _ffn_kernel(*refs, tiles_per_seq, tf, down_every, mixer_out):
    if mixer_out:
        mix_ref, wmix_ref, *refs = refs
    h_ref, gain_ref, wup_ref, cw_ref, cb_ref, wd_ref, out_ref, ext_s, up_s, act_s = refs
    i = pl.program_id(0)
    first = (i % tiles_per_seq) == 0
    tm = h_ref.shape[0]
    nj = D_FF // tf

    @pl.when(i == 0)
    def _():
        ext_s[:, tm:tm + SUBLANES, :] = jnp.zeros((nj, SUBLANES, tf), F32)

    x = h_ref[...]
    if mixer_out:
        x = x + _dot(mix_ref[...], wmix_ref[...])
    xn = _rms(x, gain_ref[...]).astype(BF16)

    def produce(j):
        _conv_stage(_dot(xn, wup_ref[:, j * tf:(j + 1) * tf]), ext_s.at[j], first)
        up_s[j] = _dot(xn, wup_ref[:, D_FF + j * tf:D_FF + (j + 1) * tf])

    acc = x
    for j in range(min(LOOKAHEAD, nj)):
        produce(j)
    for j in range(nj):
        if j + LOOKAHEAD < nj:
            produce(j + LOOKAHEAD)
        cols = slice(j * tf, (j + 1) * tf)
        cw = cw_ref[:, cols]
        cb = cb_ref[:, cols]
        for r in range(0, tm, CONV_ROWS):
            conv = _conv_block(ext_s.at[j], r, cw, F_CONV) + cb
            act_s[r:r + CONV_ROWS, cols] = (_silu(conv) * up_s[j, r:r + CONV_ROWS, :]).astype(BF16)
        if (j + 1) % down_every == 0:
            kc = slice((j + 1 - down_every) * tf, (j + 1) * tf)
            acc = acc + _dot(act_s[:, kc], wd_ref[kc, :])
    out_ref[...] = acc


def _ffn_layer(h, gain, w_up, conv_w, conv_b, w_down, *, layer, seq, tm, tf, mixer=None):
    m = h.shape[0]
    nj = D_FF // tf
    kern = functools.partial(_ffn_kernel, tiles_per_seq=seq // tm, tf=tf,
                             down_every=D_MODEL // tf, mixer_out=mixer is not None)
    mixer_specs, mixer_args = [], []
    if mixer is not None:
        mix, w_mix, mix_layer = mixer
        kdim = mix.shape[1]
        mixer_specs = [pl.BlockSpec((tm, kdim), lambda i: (i, 0)),
                       _resident((kdim, D_MODEL), mix_layer)]
        mixer_args = [mix, w_mix]
    return pl.pallas_call(
        kern,
        grid=(m // tm,),
        in_specs=mixer_specs + [
            pl.BlockSpec((tm, D_MODEL), lambda i: (i, 0)),
            _resident((1, D_MODEL)),
            _resident((D_MODEL, 2 * D_FF), layer),
            _resident((F_CONV, D_FF)),
            _resident((1, D_FF)),
            _resident((D_FF, D_MODEL), layer),
        ],
        out_specs=pl.BlockSpec((tm, D_MODEL), lambda i: (i, 0)),
        out_shape=jax.ShapeDtypeStruct((m, D_MODEL), F32),
        scratch_shapes=[pltpu.VMEM((nj, tm + SUBLANES, tf), F32),
                        pltpu.VMEM((nj, tm, tf), F32),
                        pltpu.VMEM((tm, D_FF), BF16)],
        compiler_params=pltpu.CompilerParams(dimension_semantics=("arbitrary",),
                                             vmem_limit_bytes=VMEM_LIMIT),
        name="ffn_layer",
    )(*mixer_args, h, gain, w_up, conv_w, conv_b, w_down)


def _gdn_in_kernel(h_ref, gain_ref, w_ref, wba_ref, cw_ref, alog_ref, dtb_ref,
                   qkv_ref, z_ref, bg_ref, ext_s, *, tiles_per_seq, tn):
    i = pl.program_id(0)
    first = (i % tiles_per_seq) == 0
    tm = h_ref.shape[0]
    n_conv = G_QKV // tn
    n_all = G_MAIN // tn

    @pl.when(i == 0)
    def _():
        ext_s[:, tm:tm + SUBLANES, :] = jnp.zeros((n_conv, SUBLANES, tn), F32)

    xn = _rms(h_ref[...], gain_ref[...]).astype(BF16)
    ba = _dot(xn, wba_ref[...])
    lane = lax.broadcasted_iota(jnp.int32, ba.shape, 1)
    beta = 1.0 / (1.0 + jnp.exp(-ba))
    sp_in = ba + dtb_ref[...]
    softplus = jnp.maximum(sp_in, 0.0) + jnp.log1p(jnp.exp(-jnp.abs(sp_in)))
    g = -jnp.exp(alog_ref[...]) * softplus
    bg_ref[...] = jnp.where(lane < G_V, beta, jnp.where(lane < 2 * G_V, g, 0.0))

    def produce(j):
        cols = slice(j * tn, (j + 1) * tn)
        proj = _dot(xn, w_ref[:, cols])
        if j < n_conv:
            _conv_stage(proj, ext_s.at[j], first)
        else:
            z_ref[:, (j - n_conv) * tn:(j - n_conv + 1) * tn] = proj.astype(BF16)

    n_z = n_all - n_conv
    per_z = n_conv // n_z
    order = []
    for zc in range(n_z):
        order += list(range(zc * per_z, (zc + 1) * per_z)) + [n_conv + zc]
    for j in order[:LOOKAHEAD]:
        produce(j)
    for pos, j in enumerate(order):
        if pos + LOOKAHEAD < n_all:
            produce(order[pos + LOOKAHEAD])
        if j < n_conv:
            cols = slice(j * tn, (j + 1) * tn)
            cw = cw_ref[:, cols]
            for r in range(0, tm, CONV_ROWS):
                qkv_ref[r:r + CONV_ROWS, cols] = _silu(
                    _conv_block(ext_s.at[j], r, cw, G_CONV)).astype(BF16)


def _gdn_in_proj(h, gain, w_main, w_ba, conv_w, alog, dtb, *, layer, seq, tm, tn):
    m = h.shape[0]
    kern = functools.partial(_gdn_in_kernel, tiles_per_seq=seq // tm, tn=tn)
    cur_tile = lambda i: (i, 0)
    prev_tile = cur_tile
    return pl.pallas_call(
        kern,
        grid=(m // tm,),
        in_specs=[
            pl.BlockSpec((tm, D_MODEL), cur_tile),
            _resident((1, D_MODEL)),
            _resident((D_MODEL, G_MAIN), layer),
            _resident((D_MODEL, LANES)),
            _resident((G_CONV, G_QKV)),
            _resident((1, LANES)),
            _resident((1, LANES)),
        ],
        out_specs=[
            pl.BlockSpec((tm, G_QKV), prev_tile),
            pl.BlockSpec((tm, G_VAL), cur_tile),
            pl.BlockSpec((tm, LANES), cur_tile),
        ],
        out_shape=[
            jax.ShapeDtypeStruct((m, G_QKV), BF16),
            jax.ShapeDtypeStruct((m, G_VAL), BF16),
            jax.ShapeDtypeStruct((m, LANES), F32),
        ],
        scratch_shapes=[pltpu.VMEM((G_QKV // tn, tm + SUBLANES, tn), F32)],
        compiler_params=pltpu.CompilerParams(dimension_semantics=("arbitrary",),
                                             vmem_limit_bytes=VMEM_LIMIT),
        name="gdn_in_proj",
    )(h, gain, w_main, w_ba, conv_w, alog, dtb)


G_PAIR = G_V // G_QK


G_WIDE = 256
G_CPB = G_WIDE // G_CHUNK


def _to_wide(full, blk):
    out = full[0:G_CHUNK]
    for cc in range(1, G_CPB):
        out = jnp.where(blk == cc, full[cc * G_CHUNK:(cc + 1) * G_CHUNK], out)
    return out


def _col_to_wide(col, blk):
    out = col[0:G_CHUNK]
    for cc in range(1, G_CPB):
        out = jnp.where(blk == cc, col[cc * G_CHUNK:(cc + 1) * G_CHUNK], out)
    return out


def _block_diag(wide, blk):
    zero = jnp.zeros_like(wide)
    return jnp.concatenate([jnp.where(blk == cc, wide, zero) for cc in range(G_CPB)], axis=0)


def _gdn_chunk_kernel(q_ref, k_ref, v_ref, z_ref, bg_ref, bgt_ref, ng_ref, out_ref, st_s, rows_s,
                      *, tc, heads):
    hg = pl.program_id(1)
    c = pl.program_id(2)
    nblk = tc // G_WIDE

    @pl.when(c == 0)
    def _():
        st_s[...] = jnp.zeros(st_s.shape, F32)

    ri = lax.broadcasted_iota(jnp.int32, (G_WIDE, G_WIDE), 0)
    ci = lax.broadcasted_iota(jnp.int32, (G_WIDE, G_WIDE), 1)
    same = (ri // G_CHUNK) == (ci // G_CHUNK)
    ones_incl = jnp.where(jnp.logical_and(same, ci <= ri), 1.0, 0.0).astype(BF16)
    ones_inclT = jnp.where(jnp.logical_and(same, ri <= ci), 1.0, 0.0).astype(BF16)
    ones_same = jnp.where(same, 1.0, 0.0).astype(BF16)

    wi = lax.broadcasted_iota(jnp.int32, (G_CHUNK, G_WIDE), 0)
    wl = lax.broadcasted_iota(jnp.int32, (G_CHUNK, G_WIDE), 1)
    wblk = wl // G_CHUNK
    wj = wl % G_CHUNK
    incl_w = wj <= wi
    strict_w = wj < wi
    eye_w = jnp.where(wj == wi, 1.0, 0.0)
    kblk = lax.broadcasted_iota(jnp.int32, (G_DK, G_WIDE), 1) // G_CHUNK

    bg = bg_ref[...]
    bgt = bgt_ref[...]
    gc_cols, gc_rows, gt_rows = [], [], []
    for b in range(nblk):
        sl = slice(b * G_WIDE, (b + 1) * G_WIDE)
        gcc = jnp.zeros((G_WIDE, LANES), F32)
        for piece in _split3(bg[sl]):
            gcc = gcc + _dot(ones_incl, piece)
        gc_cols.append(gcc)
        gcr = jnp.zeros((2 * G_V, G_WIDE), F32)
        gtr = jnp.zeros((2 * G_V, G_WIDE), F32)
        for piece in _split3(bgt[:, sl]):
            gcr = gcr + _dot(piece, ones_inclT)
            gtr = gtr + _dot(piece, ones_same)
        gc_rows.append(gcr)
        gt_rows.append(gtr)
    gc_all = jnp.concatenate(gc_cols, axis=0)
    rows_s[0:2 * G_V, :] = bgt
    rows_s[2 * G_V:4 * G_V, :] = jnp.concatenate(gc_rows, axis=1)
    rows_s[4 * G_V:6 * G_V, :] = jnp.concatenate(gt_rows, axis=1)

    lane = lax.broadcasted_iota(jnp.int32, (tc, LANES), 1)
    ngain = ng_ref[...]

    qk_heads = []
    for qh in range(heads // G_PAIR):
        q = q_ref[:, qh * G_DK:(qh + 1) * G_DK].astype(F32)
        k = k_ref[:, qh * G_DK:(qh + 1) * G_DK].astype(F32)
        qn = q * lax.rsqrt(jnp.sum(q * q, axis=-1, keepdims=True) + EPS) * (G_DK ** -0.5)
        kn = k * lax.rsqrt(jnp.sum(k * k, axis=-1, keepdims=True) + EPS)
        qb = qn.astype(BF16)
        kb = kn.astype(BF16)
        kn_t = kn.T
        kk_w, qk_w = [], []
        for b in range(nblk):
            sl = slice(b * G_WIDE, (b + 1) * G_WIDE)
            kk_w.append(_to_wide(_dot_nt(kb[sl], kb[sl]), wblk))
            qk_w.append(_to_wide(_dot_nt(qb[sl], kb[sl]), wblk))
        qk_heads.append((qn, kn, kn_t, kk_w, qk_w))

    per_head = {}
    chains = {}

    def st_prep(hs):
        for hh in hs:
            head = hg * heads + hh
            qn, kn, kn_t, kk_w, qk_w = qk_heads[hh // G_PAIR]
            pick = lambda a, off: jnp.sum(jnp.where(lane == head + off, a, 0.0), axis=-1,
                                          keepdims=True)
            beta_c = pick(bg, 0)
            gc_c = pick(gc_all, G_V)
            beta_r = rows_s[pl.ds(head, 1), :]
            gc_r = rows_s[pl.ds(3 * G_V + head, 1), :]
            gt_r = rows_s[pl.ds(5 * G_V + head, 1), :]
            e_gc_c = jnp.exp(gc_c)
            vk = jnp.concatenate([v_ref[:, hh * G_DV:(hh + 1) * G_DV],
                                  (kn * e_gc_c).astype(BF16)], axis=1)
            per_head[hh] = dict(beta_r=beta_r, e_gt_r=jnp.exp(gt_r),
                                e_rest_r=jnp.exp(gt_r - gc_r), qd=qn * e_gc_c, vk=vk, kn_t=kn_t)
            for b in range(nblk):
                sl = slice(b * G_WIDE, (b + 1) * G_WIDE)
                decay_w = jnp.exp(jnp.where(incl_w, _col_to_wide(gc_c[sl], wblk) - gc_r[:, sl],
                                            NEG_BIG))
                n_w = jnp.where(strict_w,
                                -(_col_to_wide(beta_c[sl], wblk) * kk_w[b] * decay_w), 0.0)
                qkm = jnp.where(incl_w, qk_w[b] * decay_w, 0.0).astype(BF16)
                chains[hh, b] = dict(hh=hh, sl=sl, n_w=n_w, qkm=qkm)

    def group_chains(hs):
        return [chains[hh, b] for hh in hs for b in range(nblk)]

    def level_mask(s):
        return jnp.logical_and(wi // (2 * s) == wj // (2 * s),
                               jnp.logical_and(wi % (2 * s) >= s, wj % (2 * s) < s))

    def st_first(hs):
        for ch in group_chains(hs):
            ch["t_w"] = eye_w + jnp.where(level_mask(1), ch["n_w"], 0.0)

    def st_join(s):
        def stage(hs):
            for ch in group_chains(hs):
                c_neg = jnp.where(level_mask(s), ch["n_w"], 0.0).astype(BF16)
                ch["x_w"] = _dot(ch["t_w"].astype(BF16), _block_diag(c_neg, wblk))
        return stage

    def st_merge(hs):
        for ch in group_chains(hs):
            ch["t_w"] = ch["t_w"] + _dot(ch["x_w"].astype(BF16),
                                         _block_diag(ch["t_w"].astype(BF16), wblk))

    def st_uw(hs):
        for ch in group_chains(hs):
            ph, sl = per_head[ch["hh"]], ch["sl"]
            t_beta = (ch["t_w"] * ph["beta_r"][:, sl]).astype(BF16)
            ch["uw"] = _dot(_block_diag(t_beta, wblk), ph["vk"][sl]).astype(BF16)

    def st_big(hs):
        for ch in group_chains(hs):
            ph, sl = per_head[ch["hh"]], ch["sl"]
            kd_t = (ph["kn_t"][:, sl] * ph["e_rest_r"][:, sl]).astype(BF16)
            zero = jnp.zeros_like(kd_t)
            lhs = jnp.concatenate([_block_diag(ch["qkm"], wblk)]
                                  + [jnp.where(kblk == cc, kd_t, zero) for cc in range(G_CPB)],
                                  axis=0)
            big = _dot(lhs, ch["uw"])
            ch["qp"] = (ph["qd"][sl] - big[:G_WIDE, G_DV:]).astype(BF16)
            ch["qu"] = big[:G_WIDE, :G_DV]
            ch["wk"] = [big[G_WIDE + cc * G_DK:G_WIDE + (cc + 1) * G_DK, G_DV:].astype(BF16)
                        for cc in range(G_CPB)]
            ch["uk"] = [big[G_WIDE + cc * G_DK:G_WIDE + (cc + 1) * G_DK, :G_DV]
                        for cc in range(G_CPB)]

    def st_recur(hs):
        states = {hh: st_s[hh] for hh in hs}
        outs = {hh: [] for hh in hs}
        for b in range(nblk):
            for cc in range(G_CPB):
                rows = slice(cc * G_CHUNK, (cc + 1) * G_CHUNK)
                t0 = b * G_WIDE + cc * G_CHUNK
                for hh in hs:
                    ch = chains[hh, b]
                    xs = _dot(jnp.concatenate([ch["wk"][cc], ch["qp"][rows]], axis=0),
                              states[hh].astype(BF16))
                    outs[hh].append(xs[G_DK:] + ch["qu"][rows])
                    states[hh] = (states[hh] * per_head[hh]["e_gt_r"][:, t0:t0 + 1]
                                  - xs[:G_DK] + ch["uk"][cc])
        for hh in hs:
            st_s[hh] = states[hh]
            o = jnp.concatenate(outs[hh], axis=0)
            z = z_ref[:, hh * G_DV:(hh + 1) * G_DV].astype(F32)
            y = _rms(o, ngain) * _silu(z)
            out_ref[:, hh * G_DV:(hh + 1) * G_DV] = y.astype(BF16)

    solve = []
    s = 2
    while s < G_CHUNK:
        solve += [st_join(s), st_merge]
        s *= 2
    for stage in [st_prep, st_first] + solve + [st_uw, st_big, st_recur]:
        stage(list(range(heads)))


def _gdn_chunk(qkv, z, bg, bgt, ngain, *, batch, seq, tc, heads):
    m = qkv.shape[0]
    nc = seq // tc
    kern = functools.partial(_gdn_chunk_kernel, tc=tc, heads=heads)
    qk_w = heads // G_PAIR * G_DK
    v_w = heads * G_DV
    rowi = lambda b, g, c: b * nc + c
    return pl.pallas_call(
        kern,
        grid=(batch, G_V // heads, nc),
        in_specs=[
            pl.BlockSpec((tc, qk_w), lambda b, g, c: (rowi(b, g, c), g)),
            pl.BlockSpec((tc, qk_w), lambda b, g, c: (rowi(b, g, c), G_KEY // qk_w + g)),
            pl.BlockSpec((tc, v_w), lambda b, g, c: (rowi(b, g, c), 2 * G_KEY // v_w + g)),
            pl.BlockSpec((tc, v_w), lambda b, g, c: (rowi(b, g, c), g)),
            pl.BlockSpec((tc, LANES), lambda b, g, c: (rowi(b, g, c), 0)),
            pl.BlockSpec((None, 2 * G_V, tc), lambda b, g, c: (b, 0, c)),
            pl.BlockSpec((1, G_DV), lambda b, g, c: (0, 0)),
        ],
        out_specs=pl.BlockSpec((tc, v_w), lambda b, g, c: (rowi(b, g, c), g)),
        out_shape=jax.ShapeDtypeStruct((m, G_VAL), BF16),
        scratch_shapes=[pltpu.VMEM((heads, G_DK, G_DV), F32),
                        pltpu.VMEM((6 * G_V, tc), F32)],
        compiler_params=pltpu.CompilerParams(
            dimension_semantics=("arbitrary", "arbitrary", "arbitrary"),
            vmem_limit_bytes=VMEM_LIMIT),
        name="gdn_chunk",
    )(qkv, qkv, qkv, z, bg, bgt, ngain)


def _rope_tables(positions):
    half = ROPE_DIM // 2
    inv_freq = ROPE_THETA ** (-jnp.arange(0, ROPE_DIM, 2, dtype=F32) / ROPE_DIM)
    ang = positions.astype(F32).reshape(-1, 1) * inv_freq
    cos, sin = jnp.cos(ang), jnp.sin(ang)
    n = ang.shape[0]
    cos_h = jnp.concatenate([cos, cos, jnp.ones((n, A_HD - 2 * half), F32)], axis=1)
    sin_h = jnp.concatenate([-sin, sin, jnp.zeros((n, A_HD - 2 * half), F32)], axis=1)
    return jnp.tile(cos_h, (1, LANES // A_HD)), jnp.tile(sin_h, (1, LANES // A_HD))


def kernel(x, positions, mixer_norm, ffn_norm, attn_w_in, attn_q_gain, attn_k_gain, attn_sinks,
           attn_w_out, gdn_w_in, gdn_conv, gdn_a_log, gdn_dt_bias, gdn_norm, gdn_w_out,
           ffn_w_up, ffn_conv, ffn_conv_b, ffn_w_down):
    batch, seq, _ = x.shape
    depth = mixer_norm.shape[0]
    m = batch * seq
    attn_tile = min(1024, seq)
    row_tile = min(512, seq)
    gdn_tc = min(512, seq)

    cosf, sins = _rope_tables(positions)
    attn_w_in_b = attn_w_in.astype(BF16)
    attn_w_out_b = attn_w_out.astype(BF16)
    gdn_w_main_b = gdn_w_in.astype(BF16)
    gdn_w_out_b = gdn_w_out.astype(BF16)
    ffn_w_up_b = ffn_w_up.astype(BF16)
    ffn_w_down_b = ffn_w_down.astype(BF16)

    h = x.reshape(m, D_MODEL)
    for i in range(depth):
        j = i // 2
        gain = mixer_norm[i].reshape(1, D_MODEL)
        mixer = None
        if i % 2 == 0:
            qg = jnp.tile(attn_q_gain[j].reshape(1, A_HD), (1, LANES // A_HD)) * (A_HD ** -0.5)
            kg = jnp.tile(attn_k_gain[j].reshape(1, A_HD), (1, LANES // A_HD))
            h = _attn_layer(h, gain, attn_w_in_b, attn_w_out_b, cosf, sins, qg, kg,
                            attn_sinks[j].astype(F32), layer=j, seq=seq, tile=attn_tile)
        else:
            w_ba = jnp.pad(gdn_w_in[j][:, G_MAIN:], ((0, 0), (0, LANES - 2 * G_V))).astype(BF16)
            pad16 = lambda a: jnp.pad(a.astype(F32).reshape(1, G_V), ((0, 0), (G_V, LANES - 2 * G_V)))
            qkv, z, bg = _gdn_in_proj(h, gain, gdn_w_main_b, w_ba, gdn_conv[j],
                                      pad16(gdn_a_log[j]), pad16(gdn_dt_bias[j]),
                                      layer=j, seq=seq, tm=row_tile, tn=256)
            bgt = jnp.transpose(bg.reshape(batch, seq, LANES)[:, :, :2 * G_V], (0, 2, 1))
            o = _gdn_chunk(qkv, z, bg, bgt, gdn_norm[j].reshape(1, G_DV), batch=batch, seq=seq,
                           tc=gdn_tc, heads=8)
            mixer = (o, gdn_w_out_b, j)
        h = _ffn_layer(h, ffn_norm[i].reshape(1, D_MODEL), ffn_w_up_b, ffn_conv[i],
                       ffn_conv_b[i].reshape(1, D_FF), ffn_w_down_b,
                       layer=i, seq=seq, tm=row_tile, tf=256, mixer=mixer)
    return h.reshape(batch, seq, D_MODEL)
```
